```python
import math
import jax, jax.numpy as jnp
from jax import lax
import numpy as np

D_MODEL = 1024
BATCH = 4
SEQ = 8192
DEPTH = 1

N_MEM = 256
MIX_WIDTH = D_MODEL
CONV_WIDTH = MIX_WIDTH // 2
CONV_GROUPS = 8
CONV_K = 3
GLA_HEADS = 4
GLA_DV = (MIX_WIDTH - CONV_WIDTH) // GLA_HEADS
GLA_DK = GLA_DV // 2
GLA_K_TOTAL = GLA_HEADS * GLA_DK
GLA_V_TOTAL = GLA_HEADS * GLA_DV
GLA_LOWRANK = 16
GLA_GATE_NORM = 16.0
GLA_CHUNK = 64
XA_HEADS = 4
XA_HEAD_DIM = D_MODEL // XA_HEADS
D_FF = 4 * D_MODEL
EPS = 1e-6

SPLITS = [CONV_WIDTH, CONV_WIDTH, CONV_WIDTH,
          GLA_K_TOTAL, GLA_K_TOTAL, GLA_V_TOTAL, GLA_V_TOTAL,
          GLA_LOWRANK, GLA_LOWRANK]
W_IN_COLS = int(sum(SPLITS))

kernel_name = "hybrid_conv_gla_memxattn_encoder_block"


def rms_norm(x, g):
    xf = x.astype(jnp.float32)
    y = xf * lax.rsqrt(jnp.mean(xf * xf, axis=-1, keepdims=True) + EPS)
    return (y * g.astype(jnp.float32)).astype(x.dtype)


def split_cols(z):
    offs = np.cumsum(SPLITS)[:-1].tolist()
    return jnp.split(z, offs, axis=-1)


def short_conv_mixer(b_gate, c_gate, u, conv_w):
    h = c_gate * u
    hp = jnp.pad(h, ((0, 0), (1, 1), (0, 0)))
    y = (conv_w[0] * hp[:, :-2] + conv_w[1] * hp[:, 1:-1] + conv_w[2] * hp[:, 2:])
    return b_gate * y


def gla_chunked(q, k, v, log_a):
    bsz, s, h, dk = q.shape
    dv = v.shape[-1]
    nc = s // GLA_CHUNK

    def to_chunks(t):
        return t.reshape(bsz, nc, GLA_CHUNK, h, t.shape[-1]).transpose(0, 3, 1, 2, 4)

    q, k, v, la = to_chunks(q), to_chunks(k), to_chunks(v), to_chunks(log_a)
    b = jnp.cumsum(la, axis=3)
    q_t = q * jnp.exp(b)
    k_t = k * jnp.exp(-b)
    mask = jnp.tril(jnp.ones((GLA_CHUNK, GLA_CHUNK), dtype=bool))
    attn = jnp.einsum('bhncd,bhnsd->bhncs', q_t, k_t)
    attn = jnp.where(mask, attn, 0.0)
    o_intra = jnp.einsum('bhncs,bhnsv->bhncv', attn, v)

    g_tot = b[:, :, :, -1, :]
    k_hat = k * jnp.exp(g_tot[:, :, :, None, :] - b)
    u_chunk = jnp.einsum('bhncd,bhncv->bhndv', k_hat, v)

    def step(state, inp):
        g_n, u_n = inp
        new_state = jnp.exp(g_n)[..., None] * state + u_n
        return new_state, state

    s0 = jnp.zeros((bsz, h, dk, dv), jnp.float32)
    _, s_before = lax.scan(step, s0, (jnp.moveaxis(g_tot, 2, 0), jnp.moveaxis(u_chunk, 2, 0)))
    o_inter = jnp.einsum('bhncd,nbhdv->bhncv', q_t, s_before)
    o = o_intra + o_inter
    return o.transpose(0, 2, 3, 1, 4).reshape(bsz, s, h, dv)


def bidirectional_gla(q, k, v, la_f, la_b):
    fwd = gla_chunked(q, k, v, la_f)
    flip = lambda t: jnp.flip(t, axis=1)
    bwd = flip(gla_chunked(flip(q), flip(k), flip(v), flip(la_b)))
    diag = jnp.einsum('bshd,bshd->bsh', q, k)[..., None] * v
    return fwd + bwd - diag


def mixer_layer(x, mix_norm, w_in, conv_w, conv_norm, w_af, b_af, w_ab, b_ab, gla_norm, w_out):
    bsz, s, _ = x.shape
    h = rms_norm(x, mix_norm)
    z = h @ w_in
    cb, cc, cu, q, k, v, g, lr_f, lr_b = split_cols(z)

    y_conv = short_conv_mixer(cb, cc, cu, conv_w)
    y_conv = rms_norm(y_conv.reshape(bsz, s, CONV_GROUPS, CONV_WIDTH // CONV_GROUPS),
                      conv_norm.reshape(CONV_GROUPS, -1)).reshape(bsz, s, CONV_WIDTH)

    f32 = jnp.float32
    qh = q.astype(f32).reshape(bsz, s, GLA_HEADS, GLA_DK) * (GLA_DK ** -0.5)
    kh = k.astype(f32).reshape(bsz, s, GLA_HEADS, GLA_DK)
    vh = v.astype(f32).reshape(bsz, s, GLA_HEADS, GLA_DV)
    la_f = jax.nn.log_sigmoid((lr_f @ w_af + b_af).astype(f32)) / GLA_GATE_NORM
    la_b = jax.nn.log_sigmoid((lr_b @ w_ab + b_ab).astype(f32)) / GLA_GATE_NORM
    la_f = la_f.reshape(bsz, s, GLA_HEADS, GLA_DK)
    la_b = la_b.reshape(bsz, s, GLA_HEADS, GLA_DK)
    o = bidirectional_gla(qh, kh, vh, la_f, la_b)
    o = rms_norm(o, gla_norm).reshape(bsz, s, GLA_V_TOTAL).astype(x.dtype)
    y_gla = o * jax.nn.silu(g)

    y = jnp.concatenate([y_conv, y_gla], axis=-1)
    return y @ w_out


def memory_cross_attention(x, mem, xa_norm, mem_norm, w_xq, w_xkv, w_xo):
    bsz, s, _ = x.shape
    hq = rms_norm(x, xa_norm) @ w_xq
    kv = rms_norm(mem, mem_norm) @ w_xkv
    km, vm = jnp.split(kv, 2, axis=-1)
    qh = hq.reshape(bsz, s, XA_HEADS, XA_HEAD_DIM)
    kh = km.reshape(bsz, N_MEM, XA_HEADS, XA_HEAD_DIM)
    vh = vm.reshape(bsz, N_MEM, XA_HEADS, XA_HEAD_DIM)
    scores = jnp.einsum('bqhd,bmhd->bhqm', qh, kh).astype(jnp.float32) / math.sqrt(XA_HEAD_DIM)
    p = jax.nn.softmax(scores, axis=-1).astype(x.dtype)
    o = jnp.einsum('bhqm,bmhd->bqhd', p, vh).reshape(bsz, s, D_MODEL)
    return o @ w_xo


def sq_relu_mlp(x, mlp_norm, w_up, w_down):
    h = rms_norm(x, mlp_norm) @ w_up
    return jnp.square(jax.nn.relu(h)) @ w_down


def setup_inputs(seed: int = 0) -> dict:
    key = jax.random.key(seed)
    ks = jax.random.split(key, 24)
    L, D = DEPTH, D_MODEL
    nrm = lambda k, shape, fan_in: jax.random.normal(k, shape, jnp.float32) * (fan_in ** -0.5)
    gain = lambda k, shape: 1.0 + 0.01 * jax.random.normal(k, shape, jnp.float32)
    bias = lambda k, shape: 0.1 * jax.random.normal(k, shape, jnp.float32)
    return {
        "x": jax.random.normal(ks[0], (BATCH, SEQ, D), jnp.float32),
        "mem": jax.random.normal(ks[1], (BATCH, N_MEM, D), jnp.float32),
        "mix_norm": gain(ks[2], (L, D)),
        "w_in": nrm(ks[3], (L, D, W_IN_COLS), D),
        "conv_w": nrm(ks[4], (L, CONV_K, CONV_WIDTH), CONV_K),
        "conv_norm": gain(ks[5], (L, CONV_WIDTH)),
        "w_af": nrm(ks[6], (L, GLA_LOWRANK, GLA_K_TOTAL), GLA_LOWRANK),
        "b_af": bias(ks[7], (L, GLA_K_TOTAL)),
        "w_ab": nrm(ks[8], (L, GLA_LOWRANK, GLA_K_TOTAL), GLA_LOWRANK),
        "b_ab": bias(ks[9], (L, GLA_K_TOTAL)),
        "gla_norm": gain(ks[10], (L, GLA_DV)),
        "w_out": nrm(ks[11], (L, MIX_WIDTH, D), MIX_WIDTH),
        "xa_norm": gain(ks[12], (L, D)),
        "mem_norm": gain(ks[13], (L, D)),
        "w_xq": nrm(ks[14], (L, D, D), D),
        "w_xkv": nrm(ks[15], (L, D, 2 * D), D),
        "w_xo": nrm(ks[16], (L, D, D), D),
        "mlp_norm": gain(ks[17], (L, D)),
        "w_up": nrm(ks[18], (L, D, D_FF), D),
        "w_down": nrm(ks[19], (L, D_FF, D), D_FF),
        "final_norm": gain(ks[20], (D,)),
    }


def reference(x, mem, mix_norm, w_in, conv_w, conv_norm, w_af, b_af, w_ab, b_ab, gla_norm,
              w_out, xa_norm, mem_norm, w_xq, w_xkv, w_xo, mlp_norm, w_up, w_down, final_norm):
    for l in range(DEPTH):
        x = x + mixer_layer(x, mix_norm[l], w_in[l], conv_w[l], conv_norm[l], w_af[l], b_af[l],
                            w_ab[l], b_ab[l], gla_norm[l], w_out[l])
        x = x + memory_cross_attention(x, mem, xa_norm[l], mem_norm[l], w_xq[l], w_xkv[l], w_xo[l])
        x = x + sq_relu_mlp(x, mlp_norm[l], w_up[l], w_down[l])
    return rms_norm(x, final_norm)
```

```python
import functools

import jax
import jax.numpy as jnp
from jax import lax
from jax.experimental import pallas as pl
from jax.experimental.pallas import tpu as pltpu

F32 = jnp.float32
BF16 = jnp.bfloat16

EPS = 1e-6
D_MODEL = 1024
CONV_WIDTH = 512
CONV_GROUP = 64
GLA_HEADS = 4
GLA_DK = 64
GLA_DV = 128
GLA_K_TOTAL = GLA_HEADS * GLA_DK
GLA_V_TOTAL = GLA_HEADS * GLA_DV
GLA_LOWRANK = 16
GLA_GATE_NORM = 16.0
GLA_CHUNK = 64
N_MEM = 256
XA_HEADS = 4
XA_HEAD_DIM = D_MODEL // XA_HEADS
D_FF = 4 * D_MODEL

LANES = 128
BF16_SUBLANES = 16
VMEM_LIMIT_BYTES = 56 * 1024 * 1024

TOKEN_TILE = 512
GLA_BLOCK = 512
CUMSUM_BLOCK = 256
FF_CHUNK = 1024


def _dot(a, b):
    return jnp.dot(a, b, preferred_element_type=F32)


def _dot_nt(a, b):
    return lax.dot_general(a, b, (((1,), (1,)), ((), ())), preferred_element_type=F32)


def _dot_tn(a, b):
    return lax.dot_general(a, b, (((0,), (0,)), ((), ())), preferred_element_type=F32)


def _rms(x, gain):
    ms = jnp.mean(x * x, axis=-1, keepdims=True)
    return x * lax.rsqrt(ms + EPS) * gain


def _params(semantics):
    return pltpu.CompilerParams(dimension_semantics=semantics,
                                vmem_limit_bytes=VMEM_LIMIT_BYTES)


def _in_proj_kernel(x_ref, gain_ref, wc_ref, wg_ref, wl_ref, zc_ref, zg_ref, lr_ref):
    h = _rms(x_ref[...], gain_ref[...]).astype(BF16)
    zc_ref[...] = _dot(h, wc_ref[...]).astype(BF16)
    zg_ref[...] = _dot(h, wg_ref[...]).astype(BF16)
    lr_ref[...] = _dot(h, wl_ref[...]).astype(BF16)


def _in_proj(x2, gain, wc, wg, wl):
    n = x2.shape[0]
    const = lambda i: (0, 0)
    row = lambda i: (i, 0)
    return pl.pallas_call(
        _in_proj_kernel,
        out_shape=(jax.ShapeDtypeStruct((n, wc.shape[1]), BF16),
                   jax.ShapeDtypeStruct((n, wg.shape[1]), BF16),
                   jax.ShapeDtypeStruct((n, wl.shape[1]), BF16)),
        grid=(n // TOKEN_TILE,),
        in_specs=[pl.BlockSpec((TOKEN_TILE, D_MODEL), row),
                  pl.BlockSpec(gain.shape, const),
                  pl.BlockSpec(wc.shape, const),
                  pl.BlockSpec(wg.shape, const),
                  pl.BlockSpec(wl.shape, const)],
        out_specs=(pl.BlockSpec((TOKEN_TILE, wc.shape[1]), row),
                   pl.BlockSpec((TOKEN_TILE, wg.shape[1]), row),
                   pl.BlockSpec((TOKEN_TILE, wl.shape[1]), row)),
        compiler_params=_params(("arbitrary",)),
        name="in_proj",
    )(x2, gain, wc, wg, wl)


def _mem_kv_kernel(m_ref, gain_ref, w_ref, kv_ref):
    h = _rms(m_ref[...], gain_ref[...]).astype(BF16)
    kv_ref[...] = _dot(h, w_ref[...]).astype(BF16)


def _mem_kv(mem2, gain, w):
    n = mem2.shape[0]
    const = lambda i: (0, 0)
    row = lambda i: (i, 0)
    return pl.pallas_call(
        _mem_kv_kernel,
        out_shape=jax.ShapeDtypeStruct((n, w.shape[1]), BF16),
        grid=(n // N_MEM,),
        in_specs=[pl.BlockSpec((N_MEM, D_MODEL), row),
                  pl.BlockSpec(gain.shape, const),
                  pl.BlockSpec(w.shape, const)],
        out_specs=pl.BlockSpec((N_MEM, w.shape[1]), row),
        compiler_params=_params(("arbitrary",)),
        name="mem_kv",
    )(mem2, gain, w)


def _gla_prepare(qkv_ref, lr_ref, wa_ref, ba_ref, tri_ref, qt_s, kt_s, kh_s, dec_s, reverse):
    nchunk = GLA_BLOCK // GLA_CHUNK
    z = _dot(lr_ref[...], wa_ref[...]) + ba_ref[...]
    la = (jnp.minimum(z, 0.0) - jnp.log(1.0 + jnp.exp(-jnp.abs(z)))) * (1.0 / GLA_GATE_NORM)
    hi = la.astype(BF16)
    lo = (la - hi.astype(F32)).astype(BF16)
    tri = tri_ref[...]
    parts = []
    for r in range(GLA_BLOCK // CUMSUM_BLOCK):
        sl = slice(r * CUMSUM_BLOCK, (r + 1) * CUMSUM_BLOCK)
        p = _dot(tri, jnp.concatenate([hi[sl], lo[sl]], axis=1))
        parts.append(p[:, :GLA_K_TOTAL] + p[:, GLA_K_TOTAL:])
    b3 = jnp.concatenate(parts, axis=0).reshape(nchunk, GLA_CHUNK, GLA_K_TOTAL)
    tot = b3[:, 0:1, :] if reverse else b3[:, GLA_CHUNK - 1:GLA_CHUNK, :]
    q3 = qkv_ref[:, 0:GLA_K_TOTAL].astype(F32).reshape(nchunk, GLA_CHUNK, GLA_K_TOTAL)
    k3 = qkv_ref[:, GLA_K_TOTAL:2 * GLA_K_TOTAL].astype(F32).reshape(nchunk, GLA_CHUNK, GLA_K_TOTAL)
    shape2 = (GLA_BLOCK, GLA_K_TOTAL)
    qt_s[...] = (q3 * (jnp.exp(b3) * (GLA_DK ** -0.5))).reshape(shape2).astype(BF16)
    kt_s[...] = (k3 * jnp.exp(-b3)).reshape(shape2).astype(BF16)
    kh_s[...] = (k3 * jnp.exp(tot - b3)).reshape(shape2).astype(BF16)
    dec_s[...] = jnp.exp(tot)


def _gla_chunk(c, qkv_ref, o_ref, qt_s, kt_s, kh_s, dec_s, st_ref, reverse):
    rows = pl.ds(pl.multiple_of(c * GLA_CHUNK, GLA_CHUNK), GLA_CHUNK)
    lane = lax.broadcasted_iota(jnp.int32, (GLA_CHUNK, LANES), 1)
    row = lax.broadcasted_iota(jnp.int32, (GLA_CHUNK, LANES), 0)
    first_head = lane < GLA_DK
    src = lane & (GLA_DK - 1)
    keep = (src > row) if reverse else (src <= row)
    vlane = lax.broadcasted_iota(jnp.int32, (GLA_CHUNK, 2 * GLA_DV), 1)
    first_head_v = vlane < GLA_DV
    dec = dec_s[c]
    zero = jnp.zeros((), BF16)
    for p in range(GLA_HEADS // 2):
        ksl = slice(p * LANES, (p + 1) * LANES)
        vsl = slice(2 * GLA_K_TOTAL + p * 2 * GLA_DV, 2 * GLA_K_TOTAL + (p + 1) * 2 * GLA_DV)
        qc = qt_s[rows, ksl]
        kc = kt_s[rows, ksl]
        khc = kh_s[rows, ksl]
        vc = qkv_ref[rows, vsl]
        kbd = jnp.concatenate([jnp.where(first_head, kc, zero),
                               jnp.where(first_head, zero, kc)], axis=0)
        khbd = jnp.concatenate([jnp.where(first_head, khc, zero),
                                jnp.where(first_head, zero, khc)], axis=0)
        vbd = jnp.concatenate([jnp.where(first_head_v, vc, zero),
                               jnp.where(first_head_v, zero, vc)], axis=0)
        scores = _dot_nt(qc, kbd)
        scores = jnp.where(keep, scores, 0.0).astype(BF16)
        st = st_ref[p]
        o = _dot(scores, vbd) + _dot_nt(qc, st.astype(BF16))
        st_ref[p] = st * dec[:, ksl] + _dot_tn(vbd, khbd)
        o_ref[rows, p * 2 * GLA_DV:(p + 1) * 2 * GLA_DV] = o.astype(o_ref.dtype)


def _gla_kernel(qkv_f_ref, lr_f_ref, qkv_b_ref, lr_b_ref, waf_ref, baf_ref, wab_ref, bab_ref,
                tril_ref, triu_ref, of_ref, ob_ref,
                qt_f, kt_f, kh_f, dec_f, st_f, qt_b, kt_b, kh_b, dec_b, st_b):
    nchunk = GLA_BLOCK // GLA_CHUNK

    @pl.when(pl.program_id(1) == 0)
    def _():
        st_f[...] = jnp.zeros_like(st_f)
        st_b[...] = jnp.zeros_like(st_b)

    _gla_prepare(qkv_f_ref, lr_f_ref, waf_ref, baf_ref, tril_ref, qt_f, kt_f, kh_f, dec_f, False)
    _gla_prepare(qkv_b_ref, lr_b_ref, wab_ref, bab_ref, triu_ref, qt_b, kt_b, kh_b, dec_b, True)

    def body(c, carry):
        _gla_chunk(c, qkv_f_ref, of_ref, qt_f, kt_f, kh_f, dec_f, st_f, False)
        _gla_chunk(nchunk - 1 - c, qkv_b_ref, ob_ref, qt_b, kt_b, kh_b, dec_b, st_b, True)
        return carry

    lax.fori_loop(0, nchunk, body, 0)


def _gla(zg, lr, waf, baf, wab, bab, tril, triu, batch, seq):
    n = zg.shape[0]
    nb = seq // GLA_BLOCK
    nchunk = GLA_BLOCK // GLA_CHUNK
    qkv_cols = 2 * GLA_K_TOTAL + GLA_V_TOTAL
    fwd = lambda b, i: (b * nb + i, 0)
    bwd = lambda b, i: (b * nb + nb - 1 - i, 0)
    const = lambda b, i: (0, 0)
    dir_scratch = [pltpu.VMEM((GLA_BLOCK, GLA_K_TOTAL), BF16),
                   pltpu.VMEM((GLA_BLOCK, GLA_K_TOTAL), BF16),
                   pltpu.VMEM((GLA_BLOCK, GLA_K_TOTAL), BF16),
                   pltpu.VMEM((nchunk, 1, GLA_K_TOTAL), F32),
                   pltpu.VMEM((GLA_HEADS // 2, 2 * GLA_DV, LANES), F32)]
    return pl.pallas_call(
        _gla_kernel,
        out_shape=(jax.ShapeDtypeStruct((n, GLA_V_TOTAL), BF16),
                   jax.ShapeDtypeStruct((n, GLA_V_TOTAL), BF16)),
        grid=(batch, nb),
        in_specs=[pl.BlockSpec((GLA_BLOCK, qkv_cols), fwd),
                  pl.BlockSpec((GLA_BLOCK, lr.shape[1]), fwd),
                  pl.BlockSpec((GLA_BLOCK, qkv_cols), bwd),
                  pl.BlockSpec((GLA_BLOCK, lr.shape[1]), bwd),
                  pl.BlockSpec(waf.shape, const),
                  pl.BlockSpec(baf.shape, const),
                  pl.BlockSpec(wab.shape, const),
                  pl.BlockSpec(bab.shape, const),
                  pl.BlockSpec(tril.shape, const),
                  pl.BlockSpec(triu.shape, const)],
        out_specs=(pl.BlockSpec((GLA_BLOCK, GLA_V_TOTAL), fwd),
                   pl.BlockSpec((GLA_BLOCK, GLA_V_TOTAL), bwd)),
        scratch_shapes=dir_scratch + dir_scratch,
        compiler_params=_params(("arbitrary", "arbitrary")),
        name="gla",
    )(zg, lr, zg, lr, waf, baf, wab, bab, tril, triu)


def _mix_xattn_kernel(tiles_per_seq, x_ref, zc_ref, prev_ref, next_ref, g_ref, of_ref, ob_ref,
                      kv_ref, convw_ref, convg_ref, gmat_ref, glag_ref, wout_ref,
                      xag_ref, wxq_ref, wxo_ref, o_ref):
    tm = x_ref.shape[0]
    t = pl.program_id(0) % tiles_per_seq
    has_prev = (t > 0).astype(F32)
    has_next = (t < tiles_per_seq - 1).astype(F32)

    h = zc_ref[:, 0:CONV_WIDTH].astype(F32) * zc_ref[:, CONV_WIDTH:2 * CONV_WIDTH].astype(F32)
    last = BF16_SUBLANES - 1
    h_before = (prev_ref[last:last + 1, 0:CONV_WIDTH].astype(F32)
                * prev_ref[last:last + 1, CONV_WIDTH:2 * CONV_WIDTH].astype(F32)) * has_prev
    h_after = (next_ref[0:1, 0:CONV_WIDTH].astype(F32)
               * next_ref[0:1, CONV_WIDTH:2 * CONV_WIDTH].astype(F32)) * has_next
    rowid = lax.broadcasted_iota(jnp.int32, (tm, CONV_WIDTH), 0)
    h_m1 = jnp.where(rowid == 0, h_before, pltpu.roll(h, 1, 0))
    h_p1 = jnp.where(rowid == tm - 1, h_after, pltpu.roll(h, tm - 1, 0))
    y = convw_ref[0:1, :] * h_m1 + convw_ref[1:2, :] * h + convw_ref[2:3, :] * h_p1
    y = zc_ref[:, 2 * CONV_WIDTH:3 * CONV_WIDTH].astype(F32) * y
    ms = _dot((y * y).astype(BF16), gmat_ref[...])
    y = y * lax.rsqrt(ms + EPS) * convg_ref[...]

    o = of_ref[...].astype(F32) + ob_ref[...].astype(F32)
    heads = []
    for hd in range(GLA_HEADS):
        oh = o[:, hd * GLA_DV:(hd + 1) * GLA_DV]
        heads.append(_rms(oh, glag_ref[...]))
    o = jnp.concatenate(heads, axis=1)
    g = g_ref[...].astype(F32)
    o = o * (g / (1.0 + jnp.exp(-g)))

    mixed = jnp.concatenate([y.astype(BF16), o.astype(BF16)], axis=1)
    x1 = x_ref[...] + _dot(mixed, wout_ref[...])

    q = _dot(_rms(x1, xag_ref[...]).astype(BF16), wxq_ref[...]).astype(BF16)
    outs = []
    for hd in range(XA_HEADS):
        sl = slice(hd * XA_HEAD_DIM, (hd + 1) * XA_HEAD_DIM)
        s = _dot_nt(q[:, sl], kv_ref[:, sl]) * (XA_HEAD_DIM ** -0.5)
        s = s - jnp.max(s, axis=-1, keepdims=True)
        e = jnp.exp(s)
        p = e / jnp.sum(e, axis=-1, keepdims=True)
        vsl = slice(D_MODEL + hd * XA_HEAD_DIM, D_MODEL + (hd + 1) * XA_HEAD_DIM)
        outs.append(_dot(p.astype(BF16), kv_ref[:, vsl]).astype(BF16))
    att = jnp.concatenate(outs, axis=1)
    o_ref[...] = x1 + _dot(att, wxo_ref[...])


def _mix_xattn(x2, zc, zg, o_f, o_b, kv, convw, convg, gmat, glag, wout, xag, wxq, wxo, seq):
    n = x2.shape[0]
    tm = TOKEN_TILE
    tiles_per_seq = seq // tm
    halo_per_tile = tm // BF16_SUBLANES
    n_halo = n // BF16_SUBLANES
    const = lambda i: (0, 0)
    row = lambda i: (i, 0)
    return pl.pallas_call(
        functools.partial(_mix_xattn_kernel, tiles_per_seq),
        out_shape=jax.ShapeDtypeStruct((n, D_MODEL), F32),
        grid=(n // tm,),
        in_specs=[pl.BlockSpec((tm, D_MODEL), row),
                  pl.BlockSpec((tm, 3 * CONV_WIDTH), row),
                  pl.BlockSpec((BF16_SUBLANES, 2 * CONV_WIDTH),
                               lambda i: (jnp.maximum(i * halo_per_tile - 1, 0), 0)),
                  pl.BlockSpec((BF16_SUBLANES, 2 * CONV_WIDTH),
                               lambda i: (jnp.minimum((i + 1) * halo_per_tile, n_halo - 1), 0)),
                  pl.BlockSpec((tm, GLA_V_TOTAL), lambda i: (i, 2)),
                  pl.BlockSpec((tm, GLA_V_TOTAL), row),
                  pl.BlockSpec((tm, GLA_V_TOTAL), row),
                  pl.BlockSpec((N_MEM, 2 * D_MODEL), lambda i: (i // tiles_per_seq, 0)),
                  pl.BlockSpec(convw.shape, const),
                  pl.BlockSpec(convg.shape, const),
                  pl.BlockSpec(gmat.shape, const),
                  pl.BlockSpec(glag.shape, const),
                  pl.BlockSpec(wout.shape, const),
                  pl.BlockSpec(xag.shape, const),
                  pl.BlockSpec(wxq.shape, const),
                  pl.BlockSpec(wxo.shape, const)],
        out_specs=pl.BlockSpec((tm, D_MODEL), row),
        compiler_params=_params(("arbitrary",)),
        name="mix_xattn",
    )(x2, zc, zc, zc, zg, o_f, o_b, kv, convw, convg, gmat, glag, wout, xag, wxq, wxo)


def _mlp_kernel(x_ref, gain_ref, wu_ref, wd_ref, fgain_ref, o_ref):
    x = x_ref[...]
    h = _rms(x, gain_ref[...]).astype(BF16)
    acc = x
    for c in range(D_FF // FF_CHUNK):
        sl = slice(c * FF_CHUNK, (c + 1) * FF_CHUNK)
        u = jnp.maximum(_dot(h, wu_ref[:, sl]), 0.0)
        acc = acc + _dot((u * u).astype(BF16), wd_ref[sl, :])
    o_ref[...] = _rms(acc, fgain_ref[...])


def _mlp(x2, gain, wu, wd, fgain):
    n = x2.shape[0]
    const = lambda i: (0, 0)
    row = lambda i: (i, 0)
    return pl.pallas_call(
        _mlp_kernel,
        out_shape=jax.ShapeDtypeStruct((n, D_MODEL), F32),
        grid=(n // TOKEN_TILE,),
        in_specs=[pl.BlockSpec((TOKEN_TILE, D_MODEL), row),
                  pl.BlockSpec(gain.shape, const),
                  pl.BlockSpec(wu.shape, const),
                  pl.BlockSpec(wd.shape, const),
                  pl.BlockSpec(fgain.shape, const)],
        out_specs=pl.BlockSpec((TOKEN_TILE, D_MODEL), row),
        compiler_params=_params(("arbitrary",)),
        name="mlp",
    )(x2, gain, wu, wd, fgain)


def _block_diag_ones(n, block):
    i = jnp.arange(n)
    return (i[:, None] // block) == (i[None, :] // block)


def kernel(x, mem, mix_norm, w_in, conv_w, conv_norm, w_af, b_af, w_ab, b_ab, gla_norm, w_out, xa_norm, mem_norm, w_xq, w_xkv, w_xo, mlp_norm, w_up, w_down, final_norm):
    batch, seq, d = x.shape
    assert d == D_MODEL and seq % GLA_BLOCK == 0 and seq % TOKEN_TILE == 0
    assert mix_norm.shape[0] == 1, "single-layer block"
    n = batch * seq
    x2 = x.reshape(n, d)
    row2 = lambda v: v.reshape(1, -1).astype(F32)

    w = w_in[0]
    cw = CONV_WIDTH
    o_q = 3 * cw
    o_lr = o_q + 2 * GLA_K_TOTAL + 2 * GLA_V_TOTAL
    wc = jnp.concatenate([w[:, cw:2 * cw], w[:, 2 * cw:3 * cw], w[:, 0:cw]], axis=1).astype(BF16)
    wg = w[:, o_q:o_lr].astype(BF16)
    wl = jnp.pad(w[:, o_lr:], ((0, 0), (0, LANES - 2 * GLA_LOWRANK))).astype(BF16)
    waf = jnp.pad(w_af[0], ((0, LANES - GLA_LOWRANK), (0, 0))).astype(BF16)
    wab = jnp.pad(w_ab[0], ((GLA_LOWRANK, LANES - 2 * GLA_LOWRANK), (0, 0))).astype(BF16)
    ci = jnp.arange(CUMSUM_BLOCK)
    same_chunk = _block_diag_ones(CUMSUM_BLOCK, GLA_CHUNK)
    tril = (same_chunk & (ci[None, :] <= ci[:, None])).astype(BF16)
    triu = (same_chunk & (ci[None, :] >= ci[:, None])).astype(BF16)
    gmat = (_block_diag_ones(cw, CONV_GROUP).astype(F32) / CONV_GROUP).astype(BF16)

    zc, zg, lr = _in_proj(x2, row2(mix_norm[0]), wc, wg, wl)
    kv = _mem_kv(mem.reshape(batch * N_MEM, d), row2(mem_norm[0]), w_xkv[0].astype(BF16))
    o_f, o_b = _gla(zg, lr, waf, row2(b_af[0]), wab, row2(b_ab[0]), tril, triu, batch, seq)
    xa = _mix_xattn(x2, zc, zg, o_f, o_b, kv, conv_w[0].astype(F32), row2(conv_norm[0]), gmat,
                    row2(gla_norm[0]), w_out[0].astype(BF16), row2(xa_norm[0]),
                    w_xq[0].astype(BF16), w_xo[0].astype(BF16), seq)
    out = _mlp(xa, row2(mlp_norm[0]), w_up[0].astype(BF16), w_down[0].astype(BF16),
               row2(final_norm))
    return out.reshape(batch, seq, d)
```

```python
import functools

import jax
import jax.numpy as jnp
from jax import lax
from jax.experimental import pallas as pl
from jax.experimental.pallas import tpu as pltpu

F32 = jnp.float32
BF16 = jnp.bfloat16

EPS = 1e-6
D_MODEL = 1024
CONV_WIDTH = 512
CONV_GROUP = 64
GLA_HEADS = 4
GLA_DK = 64
GLA_DV = 128
GLA_K_TOTAL = GLA_HEADS * GLA_DK
GLA_V_TOTAL = GLA_HEADS * GLA_DV
GLA_LOWRANK = 16
GLA_GATE_NORM = 16.0
GLA_CHUNK = 64
N_MEM = 256
XA_HEADS = 4
XA_HEAD_DIM = D_MODEL // XA_HEADS
D_FF = 4 * D_MODEL

LANES = 128
BF16_SUBLANES = 16
VMEM_LIMIT_BYTES = 56 * 1024 * 1024

TOKEN_TILE = 512
GLA_BLOCK = 512
CUMSUM_BLOCK = 256
FF_CHUNK = 1024


def _dot(a, b):
    return jnp.dot(a, b, preferred_element_type=F32)


def _dot_nt(a, b):
    return lax.dot_general(a, b, (((1,), (1,)), ((), ())), preferred_element_type=F32)


def _dot_tn(a, b):
    return lax.dot_general(a, b, (((0,), (0,)), ((), ())), preferred_element_type=F32)


def _rms(x, gain):
    ms = jnp.mean(x * x, axis=-1, keepdims=True)
    return x * lax.rsqrt(ms + EPS) * gain


def _params(semantics):
    return pltpu.CompilerParams(dimension_semantics=semantics,
                                vmem_limit_bytes=VMEM_LIMIT_BYTES)


def _in_proj_kernel(x_ref, gain_ref, wc_ref, wg_ref, wl_ref, zc_ref, zg_ref, lr_ref):
    h = _rms(x_ref[...], gain_ref[...]).astype(BF16)
    zc_ref[...] = _dot(h, wc_ref[...]).astype(BF16)
    zg_ref[...] = _dot(h, wg_ref[...]).astype(BF16)
    lr_ref[...] = _dot(h, wl_ref[...]).astype(BF16)


def _in_proj(x2, gain, wc, wg, wl):
    n = x2.shape[0]
    const = lambda i: (0, 0)
    row = lambda i: (i, 0)
    return pl.pallas_call(
        _in_proj_kernel,
        out_shape=(jax.ShapeDtypeStruct((n, wc.shape[1]), BF16),
                   jax.ShapeDtypeStruct((n, wg.shape[1]), BF16),
                   jax.ShapeDtypeStruct((n, wl.shape[1]), BF16)),
        grid=(n // TOKEN_TILE,),
        in_specs=[pl.BlockSpec((TOKEN_TILE, D_MODEL), row),
                  pl.BlockSpec(gain.shape, const),
                  pl.BlockSpec(wc.shape, const),
                  pl.BlockSpec(wg.shape, const),
                  pl.BlockSpec(wl.shape, const)],
        out_specs=(pl.BlockSpec((TOKEN_TILE, wc.shape[1]), row),
                   pl.BlockSpec((TOKEN_TILE, wg.shape[1]), row),
                   pl.BlockSpec((TOKEN_TILE, wl.shape[1]), row)),
        compiler_params=_params(("arbitrary",)),
        name="in_proj",
    )(x2, gain, wc, wg, wl)


def _mem_kv_kernel(m_ref, gain_ref, w_ref, kv_ref):
    h = _rms(m_ref[...], gain_ref[...]).astype(BF16)
    kv_ref[...] = _dot(h, w_ref[...]).astype(BF16)


def _mem_kv(mem2, gain, w):
    n = mem2.shape[0]
    const = lambda i: (0, 0)
    row = lambda i: (i, 0)
    return pl.pallas_call(
        _mem_kv_kernel,
        out_shape=jax.ShapeDtypeStruct((n, w.shape[1]), BF16),
        grid=(n // N_MEM,),
        in_specs=[pl.BlockSpec((N_MEM, D_MODEL), row),
                  pl.BlockSpec(gain.shape, const),
                  pl.BlockSpec(w.shape, const)],
        out_specs=pl.BlockSpec((N_MEM, w.shape[1]), row),
        compiler_params=_params(("arbitrary",)),
        name="mem_kv",
    )(mem2, gain, w)


def _gla_prepare(qkv_ref, lr_ref, wa_ref, ba_ref, tri_ref, qt_s, kt_s, kh_s, dec_s, reverse):
    nchunk = GLA_BLOCK // GLA_CHUNK
    z = _dot(lr_ref[...], wa_ref[...]) + ba_ref[...]
    la = (jnp.minimum(z, 0.0) - jnp.log(1.0 + jnp.exp(-jnp.abs(z)))) * (1.0 / GLA_GATE_NORM)
    la = la.astype(BF16)
    tri = tri_ref[...]
    parts = []
    for r in range(GLA_BLOCK // CUMSUM_BLOCK):
        parts.append(_dot(tri, la[r * CUMSUM_BLOCK:(r + 1) * CUMSUM_BLOCK]))
    b3 = jnp.concatenate(parts, axis=0).reshape(nchunk, GLA_CHUNK, GLA_K_TOTAL)
    tot = b3[:, 0:1, :] if reverse else b3[:, GLA_CHUNK - 1:GLA_CHUNK, :]
    dec = jnp.exp(tot)
    q3 = qkv_ref[:, 0:GLA_K_TOTAL].reshape(nchunk, GLA_CHUNK, GLA_K_TOTAL)
    k3 = qkv_ref[:, GLA_K_TOTAL:2 * GLA_K_TOTAL].reshape(nchunk, GLA_CHUNK, GLA_K_TOTAL)
    shape2 = (GLA_BLOCK, GLA_K_TOTAL)
    kt = k3 * jnp.exp(-b3).astype(BF16)
    qt_s[...] = (q3 * jnp.exp(b3).astype(BF16)).reshape(shape2)
    kt_s[...] = kt.reshape(shape2)
    kh_s[...] = (kt * dec.astype(BF16)).reshape(shape2)
    dec_s[...] = dec


def _gla_chunk(c, qkv_ref, o_ref, qt_s, kt_s, kh_s, dec_s, st_ref, reverse):
    rows = slice(c * GLA_CHUNK, (c + 1) * GLA_CHUNK)
    lane = lax.broadcasted_iota(jnp.int32, (GLA_CHUNK, LANES), 1)
    row = lax.broadcasted_iota(jnp.int32, (GLA_CHUNK, LANES), 0)
    first_head = lane < GLA_DK
    src = lane & (GLA_DK - 1)
    keep = (src > row) if reverse else (src <= row)
    vlane = lax.broadcasted_iota(jnp.int32, (GLA_CHUNK, 2 * GLA_DV), 1)
    first_head_v = vlane < GLA_DV
    dec = dec_s[c]
    zero = jnp.zeros((), BF16)
    for p in range(GLA_HEADS // 2):
        ksl = slice(p * LANES, (p + 1) * LANES)
        osl = slice(p * 2 * GLA_DV, (p + 1) * 2 * GLA_DV)
        vsl = slice(2 * GLA_K_TOTAL + p * 2 * GLA_DV, 2 * GLA_K_TOTAL + (p + 1) * 2 * GLA_DV)
        qc = qt_s[rows, ksl]
        kc = kt_s[rows, ksl]
        khc = kh_s[rows, ksl]
        vc = qkv_ref[rows, vsl]
        kbd = jnp.concatenate([jnp.where(first_head, kc, zero),
                               jnp.where(first_head, zero, kc)], axis=0)
        khbd = jnp.concatenate([jnp.where(first_head, khc, zero),
                                jnp.where(first_head, zero, khc)], axis=0)
        vbd = jnp.concatenate([jnp.where(first_head_v, vc, zero),
                               jnp.where(first_head_v, zero, vc)], axis=0)
        scores = _dot_nt(qc, kbd)
        scores = jnp.where(keep, scores, 0.0).astype(BF16)
        st = st_ref[p]
        o = _dot(scores, vbd) + _dot_nt(qc, st.astype(BF16))
        st_ref[p] = st * dec[:, ksl] + _dot_tn(vbd, khbd)
        o_ref[rows, osl] = o.astype(o_ref.dtype)


def _gla_kernel(qkv_f_ref, lr_f_ref, qkv_b_ref, lr_b_ref, waf_ref, baf_ref, wab_ref, bab_ref,
                tril_ref, triu_ref, of_ref, ob_ref,
                qt_f, kt_f, kh_f, dec_f, st_f, qt_b, kt_b, kh_b, dec_b, st_b):
    nchunk = GLA_BLOCK // GLA_CHUNK

    @pl.when(pl.program_id(1) == 0)
    def _():
        st_f[...] = jnp.zeros_like(st_f)
        st_b[...] = jnp.zeros_like(st_b)

    _gla_prepare(qkv_f_ref, lr_f_ref, waf_ref, baf_ref, tril_ref, qt_f, kt_f, kh_f, dec_f, False)
    _gla_prepare(qkv_b_ref, lr_b_ref, wab_ref, bab_ref, triu_ref, qt_b, kt_b, kh_b, dec_b, True)

    for c in range(nchunk):
        _gla_chunk(c, qkv_f_ref, of_ref, qt_f, kt_f, kh_f, dec_f, st_f, False)
        _gla_chunk(nchunk - 1 - c, qkv_b_ref, ob_ref, qt_b, kt_b, kh_b, dec_b, st_b, True)


def _gla(zg, lr, waf, baf, wab, bab, tril, triu, batch, seq):
    n = zg.shape[0]
    nb = seq // GLA_BLOCK
    nchunk = GLA_BLOCK // GLA_CHUNK
    qkv_cols = 2 * GLA_K_TOTAL + GLA_V_TOTAL
    fwd = lambda b, i: (b * nb + i, 0)
    bwd = lambda b, i: (b * nb + nb - 1 - i, 0)
    const = lambda b, i: (0, 0)
    dir_scratch = [pltpu.VMEM((GLA_BLOCK, GLA_K_TOTAL), BF16),
                   pltpu.VMEM((GLA_BLOCK, GLA_K_TOTAL), BF16),
                   pltpu.VMEM((GLA_BLOCK, GLA_K_TOTAL), BF16),
                   pltpu.VMEM((nchunk, 1, GLA_K_TOTAL), F32),
                   pltpu.VMEM((GLA_HEADS // 2, 2 * GLA_DV, LANES), F32)]
    return pl.pallas_call(
        _gla_kernel,
        out_shape=(jax.ShapeDtypeStruct((n, GLA_V_TOTAL), BF16),
                   jax.ShapeDtypeStruct((n, GLA_V_TOTAL), BF16)),
        grid=(batch, nb),
        in_specs=[pl.BlockSpec((GLA_BLOCK, qkv_cols), fwd),
                  pl.BlockSpec((GLA_BLOCK, lr.shape[1]), fwd),
                  pl.BlockSpec((GLA_BLOCK, qkv_cols), bwd),
                  pl.BlockSpec((GLA_BLOCK, lr.shape[1]), bwd),
                  pl.BlockSpec(waf.shape, const),
                  pl.BlockSpec(baf.shape, const),
                  pl.BlockSpec(wab.shape, const),
                  pl.BlockSpec(bab.shape, const),
                  pl.BlockSpec(tril.shape, const),
                  pl.BlockSpec(triu.shape, const)],
        out_specs=(pl.BlockSpec((GLA_BLOCK, GLA_V_TOTAL), fwd),
                   pl.BlockSpec((GLA_BLOCK, GLA_V_TOTAL), bwd)),
        scratch_shapes=dir_scratch + dir_scratch,
        compiler_params=_params(("arbitrary", "arbitrary")),
        name="gla",
    )(zg, lr, zg, lr, waf, baf, wab, bab, tril, triu)


def _conv_stream(ref, rows):
    return (ref[rows, 0:CONV_WIDTH].astype(F32)
            * ref[rows, CONV_WIDTH:2 * CONV_WIDTH].astype(F32))


def _mix_xattn_kernel(tiles_per_seq, x_ref, zc_ref, prev_ref, next_ref, g_ref, of_ref, ob_ref,
                      kv_ref, convw_ref, convg_ref, gmat_ref, glag_ref, wout_ref,
                      xag_ref, wxq_ref, wxo_ref, o_ref):
    tm = x_ref.shape[0]
    t = pl.program_id(0) % tiles_per_seq
    has_prev = (t > 0).astype(F32)
    has_next = (t < tiles_per_seq - 1).astype(F32)

    h = _conv_stream(zc_ref, slice(None))
    last = BF16_SUBLANES - 1
    h_before = _conv_stream(prev_ref, slice(last, last + 1)) * has_prev
    h_after = _conv_stream(next_ref, slice(0, 1)) * has_next
    rowid = lax.broadcasted_iota(jnp.int32, (tm, CONV_WIDTH), 0)
    h_m1 = jnp.where(rowid == 0, h_before, pltpu.roll(h, 1, 0))
    h_p1 = jnp.where(rowid == tm - 1, h_after, pltpu.roll(h, tm - 1, 0))
    y = convw_ref[0:1, :] * h_m1 + convw_ref[1:2, :] * h + convw_ref[2:3, :] * h_p1
    y = zc_ref[:, 2 * CONV_WIDTH:3 * CONV_WIDTH].astype(F32) * y
    ms = _dot((y * y).astype(BF16), gmat_ref[...])
    y = y * lax.rsqrt(ms + EPS) * convg_ref[...]

    o = of_ref[...].astype(F32) + ob_ref[...].astype(F32)
    heads = []
    for hd in range(GLA_HEADS):
        oh = o[:, hd * GLA_DV:(hd + 1) * GLA_DV]
        heads.append(_rms(oh, glag_ref[...]))
    o = jnp.concatenate(heads, axis=1)
    g = g_ref[...].astype(F32)
    o = o * (g / (1.0 + jnp.exp(-g)))

    mixed = jnp.concatenate([y.astype(BF16), o.astype(BF16)], axis=1)
    x1 = x_ref[...] + _dot(mixed, wout_ref[...])

    q = _dot(_rms(x1, xag_ref[...]).astype(BF16), wxq_ref[...]).astype(BF16)
    outs = []
    for hd in range(XA_HEADS):
        sl = slice(hd * XA_HEAD_DIM, (hd + 1) * XA_HEAD_DIM)
        s = _dot_nt(q[:, sl], kv_ref[:, sl]) * (XA_HEAD_DIM ** -0.5)
        s = s - jnp.max(s, axis=-1, keepdims=True)
        e = jnp.exp(s)
        p = e / jnp.sum(e, axis=-1, keepdims=True)
        vsl = slice(D_MODEL + hd * XA_HEAD_DIM, D_MODEL + (hd + 1) * XA_HEAD_DIM)
        outs.append(_dot(p.astype(BF16), kv_ref[:, vsl]).astype(BF16))
    att = jnp.concatenate(outs, axis=1)
    o_ref[...] = x1 + _dot(att, wxo_ref[...])


def _mix_xattn(x2, zc, zg, o_f, o_b, kv, convw, convg, gmat, glag, wout, xag, wxq, wxo, seq):
    n = x2.shape[0]
    tm = TOKEN_TILE
    tiles_per_seq = seq // tm
    halo_per_tile = tm // BF16_SUBLANES
    n_halo = n // BF16_SUBLANES
    const = lambda i: (0, 0)
    row = lambda i: (i, 0)
    return pl.pallas_call(
        functools.partial(_mix_xattn_kernel, tiles_per_seq),
        out_shape=jax.ShapeDtypeStruct((n, D_MODEL), F32),
        grid=(n // tm,),
        in_specs=[pl.BlockSpec((tm, D_MODEL), row),
                  pl.BlockSpec((tm, 3 * CONV_WIDTH), row),
                  pl.BlockSpec((BF16_SUBLANES, 2 * CONV_WIDTH),
                               lambda i: (jnp.maximum(i * halo_per_tile - 1, 0), 0)),
                  pl.BlockSpec((BF16_SUBLANES, 2 * CONV_WIDTH),
                               lambda i: (jnp.minimum((i + 1) * halo_per_tile, n_halo - 1), 0)),
                  pl.BlockSpec((tm, GLA_V_TOTAL), lambda i: (i, 2)),
                  pl.BlockSpec((tm, GLA_V_TOTAL), row),
                  pl.BlockSpec((tm, GLA_V_TOTAL), row),
                  pl.BlockSpec((N_MEM, 2 * D_MODEL), lambda i: (i // tiles_per_seq, 0)),
                  pl.BlockSpec(convw.shape, const),
                  pl.BlockSpec(convg.shape, const),
                  pl.BlockSpec(gmat.shape, const),
                  pl.BlockSpec(glag.shape, const),
                  pl.BlockSpec(wout.shape, const),
                  pl.BlockSpec(xag.shape, const),
                  pl.BlockSpec(wxq.shape, const),
                  pl.BlockSpec(wxo.shape, const)],
        out_specs=pl.BlockSpec((tm, D_MODEL), row),
        compiler_params=_params(("arbitrary",)),
        name="mix_xattn",
    )(x2, zc, zc, zc, zg, o_f, o_b, kv, convw, convg, gmat, glag, wout, xag, wxq, wxo)


def _mlp_kernel(x_ref, gain_ref, wu_ref, wd_ref, fgain_ref, o_ref):
    x = x_ref[...]
    h = _rms(x, gain_ref[...]).astype(BF16)
    acc = x
    for c in range(D_FF // FF_CHUNK):
        sl = slice(c * FF_CHUNK, (c + 1) * FF_CHUNK)
        u = jnp.maximum(_dot(h, wu_ref[:, sl]), 0.0)
        acc = acc + _dot((u * u).astype(BF16), wd_ref[sl, :])
    o_ref[...] = _rms(acc, fgain_ref[...])


def _mlp(x2, gain, wu, wd, fgain):
    n = x2.shape[0]
    const = lambda i: (0, 0)
    row = lambda i: (i, 0)
    return pl.pallas_call(
        _mlp_kernel,
        out_shape=jax.ShapeDtypeStruct((n, D_MODEL), F32),
        grid=(n // TOKEN_TILE,),
        in_specs=[pl.BlockSpec((TOKEN_TILE, D_MODEL), row),
                  pl.BlockSpec(gain.shape, const),
                  pl.BlockSpec(wu.shape, const),
                  pl.BlockSpec(wd.shape, const),
                  pl.BlockSpec(fgain.shape, const)],
        out_specs=pl.BlockSpec((TOKEN_TILE, D_MODEL), row),
        compiler_params=_params(("arbitrary",)),
        name="mlp",
    )(x2, gain, wu, wd, fgain)


def _block_diag_ones(n, block):
    i = jnp.arange(n)
    return (i[:, None] // block) == (i[None, :] // block)


def kernel(x, mem, mix_norm, w_in, conv_w, conv_norm, w_af, b_af, w_ab, b_ab, gla_norm, w_out, xa_norm, mem_norm, w_xq, w_xkv, w_xo, mlp_norm, w_up, w_down, final_norm):
    batch, seq, d = x.shape
    assert d == D_MODEL and seq % GLA_BLOCK == 0 and seq % TOKEN_TILE == 0
    assert mix_norm.shape[0] == 1, "single-layer block"
    n = batch * seq
    x2 = x.reshape(n, d)
    row2 = lambda v: v.reshape(1, -1).astype(F32)

    w = w_in[0]
    cw = CONV_WIDTH
    o_q = 3 * cw
    o_lr = o_q + 2 * GLA_K_TOTAL + 2 * GLA_V_TOTAL
    wc = jnp.concatenate([w[:, cw:2 * cw], w[:, 2 * cw:3 * cw], w[:, 0:cw]], axis=1).astype(BF16)
    q_scale = jnp.where(jnp.arange(o_lr - o_q) < GLA_K_TOTAL, GLA_DK ** -0.5, 1.0)
    wg = (w[:, o_q:o_lr] * q_scale).astype(BF16)
    wl = jnp.pad(w[:, o_lr:], ((0, 0), (0, LANES - 2 * GLA_LOWRANK))).astype(BF16)
    waf = jnp.pad(w_af[0], ((0, LANES - GLA_LOWRANK), (0, 0))).astype(BF16)
    wab = jnp.pad(w_ab[0], ((GLA_LOWRANK, LANES - 2 * GLA_LOWRANK), (0, 0))).astype(BF16)
    ci = jnp.arange(CUMSUM_BLOCK)
    same_chunk = _block_diag_ones(CUMSUM_BLOCK, GLA_CHUNK)
    tril = (same_chunk & (ci[None, :] <= ci[:, None])).astype(BF16)
    triu = (same_chunk & (ci[None, :] >= ci[:, None])).astype(BF16)
    gmat = (_block_diag_ones(cw, CONV_GROUP).astype(F32) / CONV_GROUP).astype(BF16)

    zc, zg, lr = _in_proj(x2, row2(mix_norm[0]), wc, wg, wl)
    kv = _mem_kv(mem.reshape(batch * N_MEM, d), row2(mem_norm[0]), w_xkv[0].astype(BF16))
    o_f, o_b = _gla(zg, lr, waf, row2(b_af[0]), wab, row2(b_ab[0]), tril, triu, batch, seq)
    xa = _mix_xattn(x2, zc, zg, o_f, o_b, kv, conv_w[0].astype(F32), row2(conv_norm[0]), gmat,
                    row2(gla_norm[0]), w_out[0].astype(BF16), row2(xa_norm[0]),
                    w_xq[0].astype(BF16), w_xo[0].astype(BF16), seq)
    out = _mlp(xa, row2(mlp_norm[0]), w_up[0].astype(BF16), w_down[0].astype(BF16),
               row2(final_norm))
    return out.reshape(batch, seq, d)
```

```python
import functools

import jax
import jax.numpy as jnp
from jax import lax
from jax.experimental import pallas as pl
from jax.experimental.pallas import tpu as pltpu

F32 = jnp.float32
BF16 = jnp.bfloat16

EPS = 1e-6
LOG2_E = 1.4426950408889634
D_MODEL = 1024
CONV_WIDTH = 512
CONV_GROUP = 64
GLA_HEADS = 4
GLA_DK = 64
GLA_DV = 128
GLA_K_TOTAL = GLA_HEADS * GLA_DK
GLA_V_TOTAL = GLA_HEADS * GLA_DV
GLA_LOWRANK = 16
GLA_GATE_NORM = 16.0
GLA_CHUNK = 64
N_MEM = 256
XA_HEADS = 4
XA_HEAD_DIM = D_MODEL // XA_HEADS
D_FF = 4 * D_MODEL

LANES = 128
BF16_SUBLANES = 16
VMEM_LIMIT_BYTES = 56 * 1024 * 1024

TOKEN_TILE = 1024
MIX_TILE = 1024
MLP_TILE = 1024
GLA_BLOCK = 1024
CUMSUM_BLOCK = 256
FF_CHUNK = 1024


def _dot(a, b):
    return jnp.dot(a, b, preferred_element_type=F32)


def _dot_nt(a, b):
    return lax.dot_general(a, b, (((1,), (1,)), ((), ())), preferred_element_type=F32)


def _dot_tn(a, b):
    return lax.dot_general(a, b, (((0,), (0,)), ((), ())), preferred_element_type=F32)


def _inv_rms(x):
    return lax.rsqrt(jnp.mean(x * x, axis=-1, keepdims=True) + EPS)


def _rms(x, gain):
    return x * _inv_rms(x) * gain


def _params(semantics):
    return pltpu.CompilerParams(dimension_semantics=semantics,
                                vmem_limit_bytes=VMEM_LIMIT_BYTES)


def _in_proj_kernel(x_ref, gain_ref, wc_ref, wg_ref, wl_ref, zc_ref, zg_ref, lr_ref):
    x = x_ref[...]
    r = _inv_rms(x)
    xg = (x * gain_ref[...]).astype(BF16)
    zc_ref[...] = (_dot(xg, wc_ref[...]) * r).astype(BF16)
    zg_ref[...] = (_dot(xg, wg_ref[...]) * r).astype(BF16)
    lr_ref[...] = (_dot(xg, wl_ref[...]) * r).astype(BF16)


def _in_proj(x2, gain, wc, wg, wl):
    n = x2.shape[0]
    const = lambda i: (0, 0)
    row = lambda i: (i, 0)
    return pl.pallas_call(
        _in_proj_kernel,
        out_shape=(jax.ShapeDtypeStruct((n, wc.shape[1]), BF16),
                   jax.ShapeDtypeStruct((n, wg.shape[1]), BF16),
                   jax.ShapeDtypeStruct((n, wl.shape[1]), BF16)),
        grid=(n // TOKEN_TILE,),
        in_specs=[pl.BlockSpec((TOKEN_TILE, D_MODEL), row),
                  pl.BlockSpec(gain.shape, const),
                  pl.BlockSpec(wc.shape, const),
                  pl.BlockSpec(wg.shape, const),
                  pl.BlockSpec(wl.shape, const)],
        out_specs=(pl.BlockSpec((TOKEN_TILE, wc.shape[1]), row),
                   pl.BlockSpec((TOKEN_TILE, wg.shape[1]), row),
                   pl.BlockSpec((TOKEN_TILE, wl.shape[1]), row)),
        compiler_params=_params(("arbitrary",)),
        name="in_proj",
    )(x2, gain, wc, wg, wl)


def _mem_kv_kernel(m_ref, gain_ref, w_ref, kv_ref):
    h = _rms(m_ref[...], gain_ref[...]).astype(BF16)
    kv_ref[...] = _dot(h, w_ref[...]).astype(BF16)


def _mem_kv(mem2, gain, w):
    n = mem2.shape[0]
    const = lambda i: (0, 0)
    row = lambda i: (i, 0)
    return pl.pallas_call(
        _mem_kv_kernel,
        out_shape=jax.ShapeDtypeStruct((n, w.shape[1]), BF16),
        grid=(n // N_MEM,),
        in_specs=[pl.BlockSpec((N_MEM, D_MODEL), row),
                  pl.BlockSpec(gain.shape, const),
                  pl.BlockSpec(w.shape, const)],
        out_specs=pl.BlockSpec((N_MEM, w.shape[1]), row),
        compiler_params=_params(("arbitrary",)),
        name="mem_kv",
    )(mem2, gain, w)


def _gla_prepare(qkv_ref, lr_ref, wa_ref, ba_ref, tri_ref, qt_s, kt_s, kh_s, dec_s, reverse):
    nchunk = GLA_BLOCK // GLA_CHUNK
    z = _dot(lr_ref[...], wa_ref[...]) + ba_ref[...]
    la = (jnp.minimum(z, 0.0) - jnp.log(1.0 + jnp.exp(-jnp.abs(z)))) * (1.0 / GLA_GATE_NORM)
    la = la.astype(BF16)
    tri = tri_ref[...]
    parts = []
    for r in range(GLA_BLOCK // CUMSUM_BLOCK):
        parts.append(_dot(tri, la[r * CUMSUM_BLOCK:(r + 1) * CUMSUM_BLOCK]))
    b3 = jnp.concatenate(parts, axis=0).reshape(nchunk, GLA_CHUNK, GLA_K_TOTAL)
    tot = b3[:, 0:1, :] if reverse else b3[:, GLA_CHUNK - 1:GLA_CHUNK, :]
    dec = jnp.exp(tot)
    q3 = qkv_ref[:, 0:GLA_K_TOTAL].reshape(nchunk, GLA_CHUNK, GLA_K_TOTAL)
    k3 = qkv_ref[:, GLA_K_TOTAL:2 * GLA_K_TOTAL].reshape(nchunk, GLA_CHUNK, GLA_K_TOTAL)
    shape2 = (GLA_BLOCK, GLA_K_TOTAL)
    kt = k3 * jnp.exp(-b3).astype(BF16)
    qt_s[...] = (q3 * jnp.exp(b3).astype(BF16)).reshape(shape2)
    kt_s[...] = kt.reshape(shape2)
    kh_s[...] = (kt * dec.astype(BF16)).reshape(shape2)
    dec_s[...] = dec


def _gla_chunk(c, qkv_ref, o_ref, qt_s, kt_s, kh_s, dec_s, st_ref, reverse):
    rows = slice(c * GLA_CHUNK, (c + 1) * GLA_CHUNK)
    lane = lax.broadcasted_iota(jnp.int32, (GLA_CHUNK, LANES), 1)
    row = lax.broadcasted_iota(jnp.int32, (GLA_CHUNK, LANES), 0)
    first_head = lane < GLA_DK
    src = lane & (GLA_DK - 1)
    keep = (src > row) if reverse else (src <= row)
    vlane = lax.broadcasted_iota(jnp.int32, (GLA_CHUNK, 2 * GLA_DV), 1)
    first_head_v = vlane < GLA_DV
    dec = dec_s[c]
    zero = jnp.zeros((), BF16)
    for p in range(GLA_HEADS // 2):
        ksl = slice(p * LANES, (p + 1) * LANES)
        osl = slice(p * 2 * GLA_DV, (p + 1) * 2 * GLA_DV)
        vsl = slice(2 * GLA_K_TOTAL + p * 2 * GLA_DV, 2 * GLA_K_TOTAL + (p + 1) * 2 * GLA_DV)
        qc = qt_s[rows, ksl]
        kc = kt_s[rows, ksl]
        khc = kh_s[rows, ksl]
        vc = qkv_ref[rows, vsl]
        kbd = jnp.concatenate([jnp.where(first_head, kc, zero),
                               jnp.where(first_head, zero, kc)], axis=0)
        khbd = jnp.concatenate([jnp.where(first_head, khc, zero),
                                jnp.where(first_head, zero, khc)], axis=0)
        vbd = jnp.concatenate([jnp.where(first_head_v, vc, zero),
                               jnp.where(first_head_v, zero, vc)], axis=0)
        scores = _dot_nt(qc, kbd)
        scores = jnp.where(keep, scores, 0.0).astype(BF16)
        st = st_ref[p]
        o = _dot(scores, vbd) + _dot_nt(qc, st.astype(BF16))
        st_ref[p] = st * dec[:, ksl] + _dot_tn(vbd, khbd)
        o_ref[rows, osl] = o.astype(o_ref.dtype)


def _gla_kernel(qkv_f_ref, lr_f_ref, qkv_b_ref, lr_b_ref, waf_ref, baf_ref, wab_ref, bab_ref,
                tril_ref, triu_ref, of_ref, ob_ref,
                qt_f, kt_f, kh_f, dec_f, st_f, qt_b, kt_b, kh_b, dec_b, st_b):
    nchunk = GLA_BLOCK // GLA_CHUNK

    @pl.when(pl.program_id(1) == 0)
    def _():
        st_f[...] = jnp.zeros_like(st_f)
        st_b[...] = jnp.zeros_like(st_b)

    _gla_prepare(qkv_f_ref, lr_f_ref, waf_ref, baf_ref, tril_ref, qt_f, kt_f, kh_f, dec_f, False)
    _gla_prepare(qkv_b_ref, lr_b_ref, wab_ref, bab_ref, triu_ref, qt_b, kt_b, kh_b, dec_b, True)

    for c in range(nchunk):
        _gla_chunk(c, qkv_f_ref, of_ref, qt_f, kt_f, kh_f, dec_f, st_f, False)
        _gla_chunk(nchunk - 1 - c, qkv_b_ref, ob_ref, qt_b, kt_b, kh_b, dec_b, st_b, True)


def _gla(zg, lr, waf, baf, wab, bab, tril, triu, batch, seq):
    n = zg.shape[0]
    nb = seq // GLA_BLOCK
    nchunk = GLA_BLOCK // GLA_CHUNK
    qkv_cols = 2 * GLA_K_TOTAL + GLA_V_TOTAL
    fwd = lambda b, i: (b * nb + i, 0)
    bwd = lambda b, i: (b * nb + nb - 1 - i, 0)
    const = lambda b, i: (0, 0)
    dir_scratch = [pltpu.VMEM((GLA_BLOCK, GLA_K_TOTAL), BF16),
                   pltpu.VMEM((GLA_BLOCK, GLA_K_TOTAL), BF16),
                   pltpu.VMEM((GLA_BLOCK, GLA_K_TOTAL), BF16),
                   pltpu.VMEM((nchunk, 1, GLA_K_TOTAL), F32),
                   pltpu.VMEM((GLA_HEADS // 2, 2 * GLA_DV, LANES), F32)]
    return pl.pallas_call(
        _gla_kernel,
        out_shape=(jax.ShapeDtypeStruct((n, GLA_V_TOTAL), BF16),
                   jax.ShapeDtypeStruct((n, GLA_V_TOTAL), BF16)),
        grid=(batch, nb),
        in_specs=[pl.BlockSpec((GLA_BLOCK, qkv_cols), fwd),
                  pl.BlockSpec((GLA_BLOCK, lr.shape[1]), fwd),
                  pl.BlockSpec((GLA_BLOCK, qkv_cols), bwd),
                  pl.BlockSpec((GLA_BLOCK, lr.shape[1]), bwd),
                  pl.BlockSpec(waf.shape, const),
                  pl.BlockSpec(baf.shape, const),
                  pl.BlockSpec(wab.shape, const),
                  pl.BlockSpec(bab.shape, const),
                  pl.BlockSpec(tril.shape, const),
                  pl.BlockSpec(triu.shape, const)],
        out_specs=(pl.BlockSpec((GLA_BLOCK, GLA_V_TOTAL), fwd),
                   pl.BlockSpec((GLA_BLOCK, GLA_V_TOTAL), bwd)),
        scratch_shapes=dir_scratch + dir_scratch,
        compiler_params=_params(("arbitrary", "arbitrary")),
        name="gla",
    )(zg, lr, zg, lr, waf, baf, wab, bab, tril, triu)


def _conv_stream(ref, rows):
    return (ref[rows, 0:CONV_WIDTH].astype(F32)
            * ref[rows, CONV_WIDTH:2 * CONV_WIDTH].astype(F32))


def _mix_xattn_kernel(tiles_per_seq, x_ref, zc_ref, prev_ref, next_ref, g_ref, of_ref, ob_ref,
                      kv_ref, convw_ref, convg_ref, gmat_ref, glag_ref, wout_ref,
                      xag_ref, wxq_ref, wxo_ref, o_ref):
    tm = x_ref.shape[0]
    t = pl.program_id(0) % tiles_per_seq
    has_prev = (t > 0).astype(F32)
    has_next = (t < tiles_per_seq - 1).astype(F32)

    h = _conv_stream(zc_ref, slice(None))
    last = BF16_SUBLANES - 1
    h_before = _conv_stream(prev_ref, slice(last, last + 1)) * has_prev
    h_after = _conv_stream(next_ref, slice(0, 1)) * has_next
    rowid = lax.broadcasted_iota(jnp.int32, (tm, CONV_WIDTH), 0)
    h_m1 = jnp.where(rowid == 0, h_before, pltpu.roll(h, 1, 0))
    h_p1 = jnp.where(rowid == tm - 1, h_after, pltpu.roll(h, tm - 1, 0))
    y = convw_ref[0:1, :] * h_m1 + convw_ref[1:2, :] * h + convw_ref[2:3, :] * h_p1
    y = zc_ref[:, 2 * CONV_WIDTH:3 * CONV_WIDTH].astype(F32) * y
    ms = _dot((y * y).astype(BF16), gmat_ref[...])
    y = y * lax.rsqrt(ms + EPS) * convg_ref[...]

    o = of_ref[...].astype(F32) + ob_ref[...].astype(F32)
    heads = []
    for hd in range(GLA_HEADS):
        oh = o[:, hd * GLA_DV:(hd + 1) * GLA_DV]
        heads.append(_rms(oh, glag_ref[...]))
    o = jnp.concatenate(heads, axis=1)
    g = g_ref[...].astype(F32)
    o = o * (g / (1.0 + jnp.exp(-g)))

    mixed = jnp.concatenate([y.astype(BF16), o.astype(BF16)], axis=1)
    x1 = x_ref[...] + _dot(mixed, wout_ref[...])

    r = _inv_rms(x1) * (XA_HEAD_DIM ** -0.5 * LOG2_E)
    q = (_dot((x1 * xag_ref[...]).astype(BF16), wxq_ref[...]) * r).astype(BF16)
    outs = []
    for hd in range(XA_HEADS):
        sl = slice(hd * XA_HEAD_DIM, (hd + 1) * XA_HEAD_DIM)
        s = _dot_nt(q[:, sl], kv_ref[:, sl])
        s = s - jnp.max(s, axis=-1, keepdims=True)
        e = jnp.exp2(s)
        p = e / jnp.sum(e, axis=-1, keepdims=True)
        vsl = slice(D_MODEL + hd * XA_HEAD_DIM, D_MODEL + (hd + 1) * XA_HEAD_DIM)
        outs.append(_dot(p.astype(BF16), kv_ref[:, vsl]).astype(BF16))
    att = jnp.concatenate(outs, axis=1)
    o_ref[...] = x1 + _dot(att, wxo_ref[...])


def _mix_xattn(x2, zc, zg, o_f, o_b, kv, convw, convg, gmat, glag, wout, xag, wxq, wxo, seq):
    n = x2.shape[0]
    tm = MIX_TILE
    tiles_per_seq = seq // tm
    halo_per_tile = tm // BF16_SUBLANES
    n_halo = n // BF16_SUBLANES
    const = lambda i: (0, 0)
    row = lambda i: (i, 0)
    return pl.pallas_call(
        functools.partial(_mix_xattn_kernel, tiles_per_seq),
        out_shape=jax.ShapeDtypeStruct((n, D_MODEL), F32),
        grid=(n // tm,),
        in_specs=[pl.BlockSpec((tm, D_MODEL), row),
                  pl.BlockSpec((tm, 3 * CONV_WIDTH), row),
                  pl.BlockSpec((BF16_SUBLANES, 2 * CONV_WIDTH),
                               lambda i: (jnp.maximum(i * halo_per_tile - 1, 0), 0)),
                  pl.BlockSpec((BF16_SUBLANES, 2 * CONV_WIDTH),
                               lambda i: (jnp.minimum((i + 1) * halo_per_tile, n_halo - 1), 0)),
                  pl.BlockSpec((tm, GLA_V_TOTAL), lambda i: (i, 2)),
                  pl.BlockSpec((tm, GLA_V_TOTAL), row),
                  pl.BlockSpec((tm, GLA_V_TOTAL), row),
                  pl.BlockSpec((N_MEM, 2 * D_MODEL), lambda i: (i // tiles_per_seq, 0)),
                  pl.BlockSpec(convw.shape, const),
                  pl.BlockSpec(convg.shape, const),
                  pl.BlockSpec(gmat.shape, const),
                  pl.BlockSpec(glag.shape, const),
                  pl.BlockSpec(wout.shape, const),
                  pl.BlockSpec(xag.shape, const),
                  pl.BlockSpec(wxq.shape, const),
                  pl.BlockSpec(wxo.shape, const)],
        out_specs=pl.BlockSpec((tm, D_MODEL), row),
        compiler_params=_params(("arbitrary",)),
        name="mix_xattn",
    )(x2, zc, zc, zc, zg, o_f, o_b, kv, convw, convg, gmat, glag, wout, xag, wxq, wxo)


def _mlp_kernel(x_ref, gain_ref, wu_ref, wd_ref, fgain_ref, o_ref):
    x = x_ref[...]
    r = _inv_rms(x)
    xg = (x * gain_ref[...]).astype(BF16)
    acc = None
    for c in range(D_FF // FF_CHUNK):
        sl = slice(c * FF_CHUNK, (c + 1) * FF_CHUNK)
        u = jnp.maximum(_dot(xg, wu_ref[:, sl]), 0.0)
        part = _dot((u * u).astype(BF16), wd_ref[sl, :])
        acc = part if acc is None else acc + part
    o_ref[...] = _rms(x + acc * (r * r), fgain_ref[...])


def _mlp(x2, gain, wu, wd, fgain):
    n = x2.shape[0]
    const = lambda i: (0, 0)
    row = lambda i: (i, 0)
    resident = lambda shape: pl.BlockSpec(shape, const, pipeline_mode=pl.Buffered(1))
    return pl.pallas_call(
        _mlp_kernel,
        out_shape=jax.ShapeDtypeStruct((n, D_MODEL), F32),
        grid=(n // MLP_TILE,),
        in_specs=[pl.BlockSpec((MLP_TILE, D_MODEL), row),
                  pl.BlockSpec(gain.shape, const),
                  resident(wu.shape),
                  resident(wd.shape),
                  pl.BlockSpec(fgain.shape, const)],
        out_specs=pl.BlockSpec((MLP_TILE, D_MODEL), row),
        compiler_params=_params(("arbitrary",)),
        name="mlp",
    )(x2, gain, wu, wd, fgain)


def _block_diag_ones(n, block):
    i = jnp.arange(n)
    return (i[:, None] // block) == (i[None, :] // block)


def kernel(x, mem, mix_norm, w_in, conv_w, conv_norm, w_af, b_af, w_ab, b_ab, gla_norm, w_out, xa_norm, mem_norm, w_xq, w_xkv, w_xo, mlp_norm, w_up, w_down, final_norm):
    batch, seq, d = x.shape
    assert d == D_MODEL and seq % GLA_BLOCK == 0 and seq % TOKEN_TILE == 0
    assert mix_norm.shape[0] == 1, "single-layer block"
    n = batch * seq
    x2 = x.reshape(n, d)
    row2 = lambda v: v.reshape(1, -1).astype(F32)

    w = w_in[0]
    cw = CONV_WIDTH
    o_q = 3 * cw
    o_lr = o_q + 2 * GLA_K_TOTAL + 2 * GLA_V_TOTAL
    wc = jnp.concatenate([w[:, cw:2 * cw], w[:, 2 * cw:3 * cw], w[:, 0:cw]], axis=1).astype(BF16)
    q_scale = jnp.where(jnp.arange(o_lr - o_q) < GLA_K_TOTAL, GLA_DK ** -0.5, 1.0)
    wg = (w[:, o_q:o_lr] * q_scale).astype(BF16)
    wl = jnp.pad(w[:, o_lr:], ((0, 0), (0, LANES - 2 * GLA_LOWRANK))).astype(BF16)
    waf = jnp.pad(w_af[0], ((0, LANES - GLA_LOWRANK), (0, 0))).astype(BF16)
    wab = jnp.pad(w_ab[0], ((GLA_LOWRANK, LANES - 2 * GLA_LOWRANK), (0, 0))).astype(BF16)
    ci = jnp.arange(CUMSUM_BLOCK)
    same_chunk = _block_diag_ones(CUMSUM_BLOCK, GLA_CHUNK)
    tril = (same_chunk & (ci[None, :] <= ci[:, None])).astype(BF16)
    triu = (same_chunk & (ci[None, :] >= ci[:, None])).astype(BF16)
    gmat = (_block_diag_ones(cw, CONV_GROUP).astype(F32) / CONV_GROUP).astype(BF16)

    zc, zg, lr = _in_proj(x2, row2(mix_norm[0]), wc, wg, wl)
    kv = _mem_kv(mem.reshape(batch * N_MEM, d), row2(mem_norm[0]), w_xkv[0].astype(BF16))
    o_f, o_b = _gla(zg, lr, waf, row2(b_af[0]), wab, row2(b_ab[0]), tril, triu, batch, seq)
    xa = _mix_xattn(x2, zc, zg, o_f, o_b, kv, conv_w[0].astype(F32), row2(conv_norm[0]), gmat,
                    row2(gla_norm[0]), w_out[0].astype(BF16), row2(xa_norm[0]),
                    w_xq[0].astype(BF16), w_xo[0].astype(BF16), seq)
    out = _mlp(xa, row2(mlp_norm[0]), w_up[0].astype(BF16), w_down[0].astype(BF16),
               row2(final_norm))
    return out.reshape(batch, seq, d)
```

```python
import functools

import jax
import jax.numpy as jnp
from jax import lax
from jax.experimental import pallas as pl
from jax.experimental.pallas import tpu as pltpu

F32 = jnp.float32
BF16 = jnp.bfloat16

EPS = 1e-6
LOG2_E = 1.4426950408889634
D_MODEL = 1024
CONV_WIDTH = 512
CONV_GROUP = 64
GLA_HEADS = 4
GLA_DK = 64
GLA_DV = 128
GLA_K_TOTAL = GLA_HEADS * GLA_DK
GLA_V_TOTAL = GLA_HEADS * GLA_DV
GLA_LOWRANK = 16
GLA_GATE_NORM = 16.0
GLA_CHUNK = 64
N_MEM = 256
XA_HEADS = 4
XA_HEAD_DIM = D_MODEL // XA_HEADS
D_FF = 4 * D_MODEL

LANES = 128
BF16_SUBLANES = 16
VMEM_LIMIT_BYTES = 56 * 1024 * 1024

TOKEN_TILE = 1024
MIX_TILE = 1024
MLP_TILE = 1024
WEIGHT_PREP_ROWS = 128
GLA_BLOCK = 1024
CUMSUM_BLOCK = 256
FF_CHUNK = 1024


def _dot(a, b):
    return jnp.dot(a, b, preferred_element_type=F32)


def _dot_nt(a, b):
    return lax.dot_general(a, b, (((1,), (1,)), ((), ())), preferred_element_type=F32)


def _dot_tn(a, b):
    return lax.dot_general(a, b, (((0,), (0,)), ((), ())), preferred_element_type=F32)


def _inv_rms(x):
    return lax.rsqrt(jnp.mean(x * x, axis=-1, keepdims=True) + EPS)


def _rms(x, gain):
    return x * _inv_rms(x) * gain


def _params(semantics):
    return pltpu.CompilerParams(dimension_semantics=semantics,
                                vmem_limit_bytes=VMEM_LIMIT_BYTES)


def _in_proj_kernel(x_ref, gain_ref, wt_ref, zc_ref, zg_ref, lr_ref, wc_ref, wg_ref, wl_ref):
    @pl.when(pl.program_id(0) == 0)
    def _():
        cw = CONV_WIDTH
        o_q = 3 * cw
        o_k = o_q + GLA_K_TOTAL
        o_lr = o_q + 2 * GLA_K_TOTAL + 2 * GLA_V_TOTAL
        def convert(dst_ref, dst0, src0, rows, scale=None):
            for r in range(0, rows, WEIGHT_PREP_ROWS):
                nr = min(WEIGHT_PREP_ROWS, rows - r)
                blk = wt_ref[src0 + r:src0 + r + nr, :]
                if scale is not None:
                    blk = blk * scale
                dst_ref[dst0 + r:dst0 + r + nr, :] = blk.astype(BF16)

        convert(wc_ref, 0, cw, 2 * cw)
        convert(wc_ref, 2 * cw, 0, cw)
        convert(wg_ref, 0, o_q, GLA_K_TOTAL, GLA_DK ** -0.5)
        convert(wg_ref, GLA_K_TOTAL, o_k, o_lr - o_k)
        wl_ref[...] = jnp.zeros_like(wl_ref)
        convert(wl_ref, 0, o_lr, 2 * GLA_LOWRANK)

    x = x_ref[...]
    r = _inv_rms(x)
    xg = (x * gain_ref[...]).astype(BF16)
    zc_ref[...] = (_dot_nt(xg, wc_ref[...]) * r).astype(BF16)
    zg_ref[...] = (_dot_nt(xg, wg_ref[...]) * r).astype(BF16)
    lr_ref[...] = (_dot_nt(xg, wl_ref[...]) * r).astype(BF16)


def _in_proj(x2, gain, w_in_t):
    n = x2.shape[0]
    const = lambda i: (0, 0)
    row = lambda i: (i, 0)
    conv_cols = 3 * CONV_WIDTH
    gla_cols = 2 * GLA_K_TOTAL + 2 * GLA_V_TOTAL
    assert w_in_t.shape == (conv_cols + gla_cols + 2 * GLA_LOWRANK, D_MODEL)
    outs = (conv_cols, gla_cols, LANES)
    return pl.pallas_call(
        _in_proj_kernel,
        out_shape=tuple(jax.ShapeDtypeStruct((n, c), BF16) for c in outs),
        grid=(n // TOKEN_TILE,),
        in_specs=[pl.BlockSpec((TOKEN_TILE, D_MODEL), row),
                  pl.BlockSpec(gain.shape, const),
                  pl.BlockSpec(w_in_t.shape, const, pipeline_mode=pl.Buffered(1))],
        out_specs=tuple(pl.BlockSpec((TOKEN_TILE, c), row) for c in outs),
        scratch_shapes=[pltpu.VMEM((c, D_MODEL), BF16) for c in outs],
        compiler_params=_params(("arbitrary",)),
        name="in_proj",
    )(x2, gain, w_in_t)


def _mem_kv_kernel(m_ref, gain_ref, w_ref, kv_ref):
    h = _rms(m_ref[...], gain_ref[...]).astype(BF16)
    kv_ref[...] = _dot(h, w_ref[...]).astype(BF16)


def _mem_kv(mem2, gain, w):
    n = mem2.shape[0]
    const = lambda i: (0, 0)
    row = lambda i: (i, 0)
    return pl.pallas_call(
        _mem_kv_kernel,
        out_shape=jax.ShapeDtypeStruct((n, w.shape[1]), BF16),
        grid=(n // N_MEM,),
        in_specs=[pl.BlockSpec((N_MEM, D_MODEL), row),
                  pl.BlockSpec(gain.shape, const),
                  pl.BlockSpec(w.shape, const)],
        out_specs=pl.BlockSpec((N_MEM, w.shape[1]), row),
        compiler_params=_params(("arbitrary",)),
        name="mem_kv",
    )(mem2, gain, w)


def _gla_prepare(qkv_ref, lr_ref, wa_ref, ba_ref, tri_ref, qt_s, kt_s, kh_s, dec_s, reverse):
    nchunk = GLA_BLOCK // GLA_CHUNK
    z = _dot(lr_ref[...], wa_ref[...]) + ba_ref[...]
    l2a = ((jnp.minimum(z, 0.0) - jnp.log(1.0 + jnp.exp(-jnp.abs(z))))
           * (LOG2_E / GLA_GATE_NORM))
    l2a = l2a.astype(BF16)
    tri = tri_ref[...]
    for r in range(GLA_BLOCK // CUMSUM_BLOCK):
        b = _dot(tri, l2a[r * CUMSUM_BLOCK:(r + 1) * CUMSUM_BLOCK])
        grow = jnp.exp2(b)
        shrink = jnp.exp2(-b)
        rows = slice(r * CUMSUM_BLOCK, (r + 1) * CUMSUM_BLOCK)
        qt_s[rows, :] = qkv_ref[rows, 0:GLA_K_TOTAL] * grow.astype(BF16)
        kt = qkv_ref[rows, GLA_K_TOTAL:2 * GLA_K_TOTAL] * shrink.astype(BF16)
        kt_s[rows, :] = kt
        for cc in range(CUMSUM_BLOCK // GLA_CHUNK):
            c = r * (CUMSUM_BLOCK // GLA_CHUNK) + cc
            last = cc * GLA_CHUNK + (0 if reverse else GLA_CHUNK - 1)
            dec = grow[last:last + 1, :]
            dec_s[c] = dec
            crow = slice(cc * GLA_CHUNK, (cc + 1) * GLA_CHUNK)
            kh_s[c * GLA_CHUNK:(c + 1) * GLA_CHUNK, :] = kt[crow] * dec.astype(BF16)


def _gla_chunk(c, qkv_ref, o_ref, qt_s, kt_s, kh_s, dec_s, st_ref, reverse):
    rows = slice(c * GLA_CHUNK, (c + 1) * GLA_CHUNK)
    lane = lax.broadcasted_iota(jnp.int32, (GLA_CHUNK, LANES), 1)
    row = lax.broadcasted_iota(jnp.int32, (GLA_CHUNK, LANES), 0)
    first_head = lane < GLA_DK
    src = lane & (GLA_DK - 1)
    keep = (src > row) if reverse else (src <= row)
    vlane = lax.broadcasted_iota(jnp.int32, (GLA_CHUNK, 2 * GLA_DV), 1)
    first_head_v = vlane < GLA_DV
    dec = dec_s[c]
    zero = jnp.zeros((), BF16)
    for p in range(GLA_HEADS // 2):
        ksl = slice(p * LANES, (p + 1) * LANES)
        osl = slice(p * 2 * GLA_DV, (p + 1) * 2 * GLA_DV)
        vsl = slice(2 * GLA_K_TOTAL + p * 2 * GLA_DV, 2 * GLA_K_TOTAL + (p + 1) * 2 * GLA_DV)
        qc = qt_s[rows, ksl]
        kc = kt_s[rows, ksl]
        khc = kh_s[rows, ksl]
        vc = qkv_ref[rows, vsl]
        kbd = jnp.concatenate([jnp.where(first_head, kc, zero),
                               jnp.where(first_head, zero, kc)], axis=0)
        khbd = jnp.concatenate([jnp.where(first_head, khc, zero),
                                jnp.where(first_head, zero, khc)], axis=0)
        vbd = jnp.concatenate([jnp.where(first_head_v, vc, zero),
                               jnp.where(first_head_v, zero, vc)], axis=0)
        scores = _dot_nt(qc, kbd)
        scores = jnp.where(keep, scores, 0.0).astype(BF16)
        st = st_ref[p]
        o = _dot(scores, vbd) + _dot_nt(qc, st.astype(BF16))
        st_ref[p] = st * dec[:, ksl] + _dot_tn(vbd, khbd)
        o_ref[rows, osl] = o.astype(o_ref.dtype)


def _gla_kernel(qkv_f_ref, lr_f_ref, qkv_b_ref, lr_b_ref, waf_ref, baf_ref, wab_ref, bab_ref,
                tril_ref, triu_ref, of_ref, ob_ref,
                qt_f, kt_f, kh_f, dec_f, st_f, qt_b, kt_b, kh_b, dec_b, st_b):
    nchunk = GLA_BLOCK // GLA_CHUNK

    @pl.when(pl.program_id(1) == 0)
    def _():
        st_f[...] = jnp.zeros_like(st_f)
        st_b[...] = jnp.zeros_like(st_b)

    _gla_prepare(qkv_f_ref, lr_f_ref, waf_ref, baf_ref, tril_ref, qt_f, kt_f, kh_f, dec_f, False)
    _gla_prepare(qkv_b_ref, lr_b_ref, wab_ref, bab_ref, triu_ref, qt_b, kt_b, kh_b, dec_b, True)

    for c in range(nchunk):
        _gla_chunk(c, qkv_f_ref, of_ref, qt_f, kt_f, kh_f, dec_f, st_f, False)
        _gla_chunk(nchunk - 1 - c, qkv_b_ref, ob_ref, qt_b, kt_b, kh_b, dec_b, st_b, True)


def _gla(zg, lr, waf, baf, wab, bab, tril, triu, batch, seq):
    n = zg.shape[0]
    nb = seq // GLA_BLOCK
    nchunk = GLA_BLOCK // GLA_CHUNK
    qkv_cols = 2 * GLA_K_TOTAL + GLA_V_TOTAL
    fwd = lambda b, i: (b * nb + i, 0)
    bwd = lambda b, i: (b * nb + nb - 1 - i, 0)
    const = lambda b, i: (0, 0)
    dir_scratch = [pltpu.VMEM((GLA_BLOCK, GLA_K_TOTAL), BF16),
                   pltpu.VMEM((GLA_BLOCK, GLA_K_TOTAL), BF16),
                   pltpu.VMEM((GLA_BLOCK, GLA_K_TOTAL), BF16),
                   pltpu.VMEM((nchunk, 1, GLA_K_TOTAL), F32),
                   pltpu.VMEM((GLA_HEADS // 2, 2 * GLA_DV, LANES), F32)]
    return pl.pallas_call(
        _gla_kernel,
        out_shape=(jax.ShapeDtypeStruct((n, GLA_V_TOTAL), BF16),
                   jax.ShapeDtypeStruct((n, GLA_V_TOTAL), BF16)),
        grid=(batch, nb),
        in_specs=[pl.BlockSpec((GLA_BLOCK, qkv_cols), fwd),
                  pl.BlockSpec((GLA_BLOCK, lr.shape[1]), fwd),
                  pl.BlockSpec((GLA_BLOCK, qkv_cols), bwd),
                  pl.BlockSpec((GLA_BLOCK, lr.shape[1]), bwd),
                  pl.BlockSpec(waf.shape, const),
                  pl.BlockSpec(baf.shape, const),
                  pl.BlockSpec(wab.shape, const),
                  pl.BlockSpec(bab.shape, const),
                  pl.BlockSpec(tril.shape, const),
                  pl.BlockSpec(triu.shape, const)],
        out_specs=(pl.BlockSpec((GLA_BLOCK, GLA_V_TOTAL), fwd),
                   pl.BlockSpec((GLA_BLOCK, GLA_V_TOTAL), bwd)),
        scratch_shapes=dir_scratch + dir_scratch,
        compiler_params=_params(("arbitrary", "arbitrary")),
        name="gla",
    )(zg, lr, zg, lr, waf, baf, wab, bab, tril, triu)


def _conv_stream(ref, rows):
    return (ref[rows, 0:CONV_WIDTH].astype(F32)
            * ref[rows, CONV_WIDTH:2 * CONV_WIDTH].astype(F32))


def _mix_xattn_kernel(tiles_per_seq, x_ref, zc_ref, prev_ref, next_ref, g_ref, of_ref, ob_ref,
                      kv_ref, convw_ref, convg_ref, gmat_ref, glag_ref, wout_ref,
                      xag_ref, wxq_ref, wxo_ref, o_ref):
    tm = x_ref.shape[0]
    t = pl.program_id(0) % tiles_per_seq
    has_prev = (t > 0).astype(F32)
    has_next = (t < tiles_per_seq - 1).astype(F32)

    h = _conv_stream(zc_ref, slice(None))
    last = BF16_SUBLANES - 1
    h_before = _conv_stream(prev_ref, slice(last, last + 1)) * has_prev
    h_after = _conv_stream(next_ref, slice(0, 1)) * has_next
    rowid = lax.broadcasted_iota(jnp.int32, (tm, CONV_WIDTH), 0)
    h_m1 = jnp.where(rowid == 0, h_before, pltpu.roll(h, 1, 0))
    h_p1 = jnp.where(rowid == tm - 1, h_after, pltpu.roll(h, tm - 1, 0))
    y = convw_ref[0:1, :] * h_m1 + convw_ref[1:2, :] * h + convw_ref[2:3, :] * h_p1
    y = zc_ref[:, 2 * CONV_WIDTH:3 * CONV_WIDTH].astype(F32) * y
    ms = _dot((y * y).astype(BF16), gmat_ref[...])
    y = y * lax.rsqrt(ms + EPS) * convg_ref[...]

    o = of_ref[...].astype(F32) + ob_ref[...].astype(F32)
    heads = []
    for hd in range(GLA_HEADS):
        oh = o[:, hd * GLA_DV:(hd + 1) * GLA_DV]
        heads.append(_rms(oh, glag_ref[...]))
    o = jnp.concatenate(heads, axis=1)
    g = g_ref[...].astype(F32)
    o = o * (g / (1.0 + jnp.exp(-g)))

    mixed = jnp.concatenate([y.astype(BF16), o.astype(BF16)], axis=1)
    x1 = x_ref[...] + _dot(mixed, wout_ref[...])

    r = _inv_rms(x1) * (XA_HEAD_DIM ** -0.5 * LOG2_E)
    q = (_dot((x1 * xag_ref[...]).astype(BF16), wxq_ref[...]) * r).astype(BF16)
    outs = []
    for hd in range(XA_HEADS):
        sl = slice(hd * XA_HEAD_DIM, (hd + 1) * XA_HEAD_DIM)
        s = _dot_nt(q[:, sl], kv_ref[:, sl])
        s = s - jnp.max(s, axis=-1, keepdims=True)
        e = jnp.exp2(s)
        p = e / jnp.sum(e, axis=-1, keepdims=True)
        vsl = slice(D_MODEL + hd * XA_HEAD_DIM, D_MODEL + (hd + 1) * XA_HEAD_DIM)
        outs.append(_dot(p.astype(BF16), kv_ref[:, vsl]).astype(BF16))
    att = jnp.concatenate(outs, axis=1)
    o_ref[...] = x1 + _dot(att, wxo_ref[...])


def _mix_xattn(x2, zc, zg, o_f, o_b, kv, convw, convg, gmat, glag, wout, xag, wxq, wxo, seq):
    n = x2.shape[0]
    tm = MIX_TILE
    tiles_per_seq = seq // tm
    halo_per_tile = tm // BF16_SUBLANES
    n_halo = n // BF16_SUBLANES
    const = lambda i: (0, 0)
    row = lambda i: (i, 0)
    return pl.pallas_call(
        functools.partial(_mix_xattn_kernel, tiles_per_seq),
        out_shape=jax.ShapeDtypeStruct((n, D_MODEL), F32),
        grid=(n // tm,),
        in_specs=[pl.BlockSpec((tm, D_MODEL), row),
                  pl.BlockSpec((tm, 3 * CONV_WIDTH), row),
                  pl.BlockSpec((BF16_SUBLANES, 2 * CONV_WIDTH),
                               lambda i: (jnp.maximum(i * halo_per_tile - 1, 0), 0)),
                  pl.BlockSpec((BF16_SUBLANES, 2 * CONV_WIDTH),
                               lambda i: (jnp.minimum((i + 1) * halo_per_tile, n_halo - 1), 0)),
                  pl.BlockSpec((tm, GLA_V_TOTAL), lambda i: (i, 2)),
                  pl.BlockSpec((tm, GLA_V_TOTAL), row),
                  pl.BlockSpec((tm, GLA_V_TOTAL), row),
                  pl.BlockSpec((N_MEM, 2 * D_MODEL), lambda i: (i // tiles_per_seq, 0)),
                  pl.BlockSpec(convw.shape, const),
                  pl.BlockSpec(convg.shape, const),
                  pl.BlockSpec(gmat.shape, const),
                  pl.BlockSpec(glag.shape, const),
                  pl.BlockSpec(wout.shape, const),
                  pl.BlockSpec(xag.shape, const),
                  pl.BlockSpec(wxq.shape, const),
                  pl.BlockSpec(wxo.shape, const)],
        out_specs=pl.BlockSpec((tm, D_MODEL), row),
        compiler_params=_params(("arbitrary",)),
        name="mix_xattn",
    )(x2, zc, zc, zc, zg, o_f, o_b, kv, convw, convg, gmat, glag, wout, xag, wxq, wxo)


def _mlp_kernel(x_ref, gain_ref, wu_ref, wd_ref, fgain_ref, o_ref):
    x = x_ref[...]
    r = _inv_rms(x)
    xg = (x * gain_ref[...]).astype(BF16)
    acc = None
    for c in range(D_FF // FF_CHUNK):
        sl = slice(c * FF_CHUNK, (c + 1) * FF_CHUNK)
        u = jnp.maximum(_dot(xg, wu_ref[:, sl]), 0.0)
        part = _dot((u * u).astype(BF16), wd_ref[sl, :])
        acc = part if acc is None else acc + part
    o_ref[...] = _rms(x + acc * (r * r), fgain_ref[...])


def _mlp(x2, gain, wu, wd, fgain):
    n = x2.shape[0]
    const = lambda i: (0, 0)
    row = lambda i: (i, 0)
    resident = lambda shape: pl.BlockSpec(shape, const, pipeline_mode=pl.Buffered(1))
    return pl.pallas_call(
        _mlp_kernel,
        out_shape=jax.ShapeDtypeStruct((n, D_MODEL), F32),
        grid=(n // MLP_TILE,),
        in_specs=[pl.BlockSpec((MLP_TILE, D_MODEL), row),
                  pl.BlockSpec(gain.shape, const),
                  resident(wu.shape),
                  resident(wd.shape),
                  pl.BlockSpec(fgain.shape, const)],
        out_specs=pl.BlockSpec((MLP_TILE, D_MODEL), row),
        compiler_params=_params(("arbitrary",)),
        name="mlp",
    )(x2, gain, wu, wd, fgain)


def _block_diag_ones(n, block):
    i = jnp.arange(n)
    return (i[:, None] // block) == (i[None, :] // block)


def kernel(x, mem, mix_norm, w_in, conv_w, conv_norm, w_af, b_af, w_ab, b_ab, gla_norm, w_out, xa_norm, mem_norm, w_xq, w_xkv, w_xo, mlp_norm, w_up, w_down, final_norm):
    batch, seq, d = x.shape
    assert d == D_MODEL and seq % GLA_BLOCK == 0 and seq % TOKEN_TILE == 0
    assert mix_norm.shape[0] == 1, "single-layer block"
    n = batch * seq
    x2 = x.reshape(n, d)
    row2 = lambda v: v.reshape(1, -1).astype(F32)

    cw = CONV_WIDTH
    waf = jnp.pad(w_af[0], ((0, LANES - GLA_LOWRANK), (0, 0))).astype(BF16)
    wab = jnp.pad(w_ab[0], ((GLA_LOWRANK, LANES - 2 * GLA_LOWRANK), (0, 0))).astype(BF16)
    ci = jnp.arange(CUMSUM_BLOCK)
    same_chunk = _block_diag_ones(CUMSUM_BLOCK, GLA_CHUNK)
    tril = (same_chunk & (ci[None, :] <= ci[:, None])).astype(BF16)
    triu = (same_chunk & (ci[None, :] >= ci[:, None])).astype(BF16)
    gmat = (_block_diag_ones(cw, CONV_GROUP).astype(F32) / CONV_GROUP).astype(BF16)

    zc, zg, lr = _in_proj(x2, row2(mix_norm[0]), w_in[0].T)
    kv = _mem_kv(mem.reshape(batch * N_MEM, d), row2(mem_norm[0]), w_xkv[0].astype(BF16))
    o_f, o_b = _gla(zg, lr, waf, row2(b_af[0]), wab, row2(b_ab[0]), tril, triu, batch, seq)
    xa = _mix_xattn(x2, zc, zg, o_f, o_b, kv, conv_w[0].astype(F32), row2(conv_norm[0]), gmat,
                    row2(gla_norm[0]), w_out[0].astype(BF16), row2(xa_norm[0]),
                    w_xq[0].astype(BF16), w_xo[0].astype(BF16), seq)
    out = _mlp(xa, row2(mlp_norm[0]), w_up[0].astype(BF16), w_down[0].astype(BF16),
               row2(final_norm))
    return out.reshape(batch, seq, d)
```

```python
import functools

import jax
import jax.numpy as jnp
from jax import lax
from jax.experimental import pallas as pl
from jax.experimental.pallas import tpu as pltpu

F32 = jnp.float32
BF16 = jnp.bfloat16

EPS = 1e-6
LOG2_E = 1.4426950408889634
D_MODEL = 1024
CONV_WIDTH = 512
CONV_GROUP = 64
GLA_HEADS = 4
GLA_DK = 64
GLA_DV = 128
GLA_K_TOTAL = GLA_HEADS * GLA_DK
GLA_V_TOTAL = GLA_HEADS * GLA_DV
GLA_LOWRANK = 16
GLA_GATE_NORM = 16.0
GLA_CHUNK = 64
N_MEM = 256
XA_HEADS = 4
XA_HEAD_DIM = D_MODEL // XA_HEADS
D_FF = 4 * D_MODEL

LANES = 128
BF16_SUBLANES = 16
VMEM_LIMIT_BYTES = 56 * 1024 * 1024

TOKEN_TILE = 1024
MIX_TILE = 1024
MLP_TILE = 1024
WEIGHT_PREP_ROWS = 128
GLA_BLOCK = 1024
CUMSUM_BLOCK = 256
FF_CHUNK = 1024


def _dot(a, b):
    return jnp.dot(a, b, preferred_element_type=F32)


def _dot_nt(a, b):
    return lax.dot_general(a, b, (((1,), (1,)), ((), ())), preferred_element_type=F32)


def _dot_tn(a, b):
    return lax.dot_general(a, b, (((0,), (0,)), ((), ())), preferred_element_type=F32)


def _inv_rms(x):
    return lax.rsqrt(jnp.mean(x * x, axis=-1, keepdims=True) + EPS)


def _rms(x, gain):
    return x * _inv_rms(x) * gain


def _ordered_after(lhs, deps):
    bits = None
    for dep in deps:
        words = pltpu.bitcast(dep, jnp.uint32)
        for r in range(0, words.shape[0], 8):
            for c in range(0, words.shape[1], LANES):
                word = words[r:r + 8, c:c + LANES]
                bits = word if bits is None else bits | word
    zero = pltpu.bitcast((bits >> 16) >> 16, F32)
    zero = jnp.concatenate([zero, zero], axis=0).astype(BF16)
    zero = jnp.concatenate([zero] * (lhs.shape[1] // LANES), axis=1)
    return jnp.concatenate([lhs[0:BF16_SUBLANES] + zero, lhs[BF16_SUBLANES:]], axis=0)


def _params(semantics):
    return pltpu.CompilerParams(dimension_semantics=semantics,
                                vmem_limit_bytes=VMEM_LIMIT_BYTES)


def _in_proj_kernel(x_ref, gain_ref, wt_ref, wa_ref, ba_ref, hb_ref, qkv_ref, gate_ref, la_ref,
                    wc_ref, wg_ref, wl_ref):
    @pl.when(pl.program_id(0) == 0)
    def _():
        cw = CONV_WIDTH
        o_q = 3 * cw
        o_k = o_q + GLA_K_TOTAL
        o_lr = o_q + 2 * GLA_K_TOTAL + 2 * GLA_V_TOTAL
        def convert(dst_ref, dst0, src0, rows, scale=None):
            for r in range(0, rows, WEIGHT_PREP_ROWS):
                nr = min(WEIGHT_PREP_ROWS, rows - r)
                blk = wt_ref[src0 + r:src0 + r + nr, :]
                if scale is not None:
                    blk = blk * scale
                dst_ref[dst0 + r:dst0 + r + nr, :] = blk.astype(BF16)

        convert(wc_ref, 0, cw, 2 * cw)
        convert(wc_ref, 2 * cw, 0, cw)
        convert(wg_ref, 0, o_q, GLA_K_TOTAL, GLA_DK ** -0.5)
        convert(wg_ref, GLA_K_TOTAL, o_k, o_lr - o_k)
        wl_ref[...] = jnp.zeros_like(wl_ref)
        convert(wl_ref, 0, o_lr, 2 * GLA_LOWRANK)

    x = x_ref[...]
    r = _inv_rms(x)
    xg = (x * gain_ref[...]).astype(BF16)
    cw = CONV_WIDTH
    qkv_cols = 2 * GLA_K_TOTAL + GLA_V_TOTAL
    zc = _dot_nt(xg, wc_ref[...])
    hb_ref[:, 0:cw] = (zc[:, 0:cw] * zc[:, cw:2 * cw] * (r * r)).astype(BF16)
    hb_ref[:, cw:2 * cw] = (zc[:, 2 * cw:3 * cw] * r).astype(BF16)
    zg = _dot_nt(xg, wg_ref[...]) * r
    qkv_ref[...] = zg[:, 0:qkv_cols].astype(BF16)
    g = zg[:, qkv_cols:]
    gate_ref[...] = (g / (1.0 + jnp.exp(-g))).astype(BF16)
    codes = (_dot_nt(xg, wl_ref[...]) * r).astype(BF16)
    z = _dot(codes, wa_ref[...]) + ba_ref[...]
    l2a = ((jnp.minimum(z, 0.0) - jnp.log(1.0 + jnp.exp(-jnp.abs(z))))
           * (LOG2_E / GLA_GATE_NORM))
    la_ref[...] = l2a.astype(BF16)


def _in_proj(x2, gain, w_in_t, wa, ba):
    n = x2.shape[0]
    const = lambda i: (0, 0)
    row = lambda i: (i, 0)
    conv_cols = 3 * CONV_WIDTH
    gla_cols = 2 * GLA_K_TOTAL + 2 * GLA_V_TOTAL
    assert w_in_t.shape == (conv_cols + gla_cols + 2 * GLA_LOWRANK, D_MODEL)
    outs = (2 * CONV_WIDTH, 2 * GLA_K_TOTAL + GLA_V_TOTAL, GLA_V_TOTAL, 2 * GLA_K_TOTAL)
    return pl.pallas_call(
        _in_proj_kernel,
        out_shape=tuple(jax.ShapeDtypeStruct((n, c), BF16) for c in outs),
        grid=(n // TOKEN_TILE,),
        in_specs=[pl.BlockSpec((TOKEN_TILE, D_MODEL), row),
                  pl.BlockSpec(gain.shape, const),
                  pl.BlockSpec(w_in_t.shape, const, pipeline_mode=pl.Buffered(1)),
                  pl.BlockSpec(wa.shape, const),
                  pl.BlockSpec(ba.shape, const)],
        out_specs=tuple(pl.BlockSpec((TOKEN_TILE, c), row) for c in outs),
        scratch_shapes=[pltpu.VMEM((c, D_MODEL), BF16) for c in (conv_cols, gla_cols, LANES)],
        compiler_params=_params(("arbitrary",)),
        name="in_proj",
    )(x2, gain, w_in_t, wa, ba)


def _mem_kv_kernel(m_ref, gain_ref, w_ref, kv_ref):
    h = _rms(m_ref[...], gain_ref[...]).astype(BF16)
    kv_ref[...] = _dot(h, w_ref[...]).astype(BF16)


def _mem_kv(mem2, gain, w):
    n = mem2.shape[0]
    const = lambda i: (0, 0)
    row = lambda i: (i, 0)
    return pl.pallas_call(
        _mem_kv_kernel,
        out_shape=jax.ShapeDtypeStruct((n, w.shape[1]), BF16),
        grid=(n // N_MEM,),
        in_specs=[pl.BlockSpec((N_MEM, D_MODEL), row),
                  pl.BlockSpec(gain.shape, const),
                  pl.BlockSpec(w.shape, const)],
        out_specs=pl.BlockSpec((N_MEM, w.shape[1]), row),
        compiler_params=_params(("arbitrary",)),
        name="mem_kv",
    )(mem2, gain, w)


def _gla_prepare(qkv_ref, la_ref, tri_ref, qt_s, kt_s, kh_s, dec_s, reverse):
    tri = tri_ref[...]
    for r in range(GLA_BLOCK // CUMSUM_BLOCK):
        b = _dot(tri, la_ref[r * CUMSUM_BLOCK:(r + 1) * CUMSUM_BLOCK, :])
        grow = jnp.exp2(b)
        shrink = jnp.exp2(-b)
        rows = slice(r * CUMSUM_BLOCK, (r + 1) * CUMSUM_BLOCK)
        qt_s[rows, :] = qkv_ref[rows, 0:GLA_K_TOTAL] * grow.astype(BF16)
        kt = qkv_ref[rows, GLA_K_TOTAL:2 * GLA_K_TOTAL] * shrink.astype(BF16)
        kt_s[rows, :] = kt
        for cc in range(CUMSUM_BLOCK // GLA_CHUNK):
            c = r * (CUMSUM_BLOCK // GLA_CHUNK) + cc
            last = cc * GLA_CHUNK + (0 if reverse else GLA_CHUNK - 1)
            dec = grow[last:last + 1, :]
            dec_s[c] = dec
            crow = slice(cc * GLA_CHUNK, (cc + 1) * GLA_CHUNK)
            kh_s[c * GLA_CHUNK:(c + 1) * GLA_CHUNK, :] = kt[crow] * dec.astype(BF16)


def _gla_chunk(c, qkv_ref, o_ref, qt_s, kt_s, kh_s, dec_s, st_ref, reverse):
    rows = slice(c * GLA_CHUNK, (c + 1) * GLA_CHUNK)
    lane = lax.broadcasted_iota(jnp.int32, (GLA_CHUNK, LANES), 1)
    row = lax.broadcasted_iota(jnp.int32, (GLA_CHUNK, LANES), 0)
    first_head = lane < GLA_DK
    src = lane & (GLA_DK - 1)
    keep = (src > row) if reverse else (src <= row)
    vlane = lax.broadcasted_iota(jnp.int32, (GLA_CHUNK, 2 * GLA_DV), 1)
    first_head_v = vlane < GLA_DV
    dec = dec_s[c]
    zero = jnp.zeros((), BF16)
    for p in range(GLA_HEADS // 2):
        ksl = slice(p * LANES, (p + 1) * LANES)
        osl = slice(p * 2 * GLA_DV, (p + 1) * 2 * GLA_DV)
        vsl = slice(2 * GLA_K_TOTAL + p * 2 * GLA_DV, 2 * GLA_K_TOTAL + (p + 1) * 2 * GLA_DV)
        qc = qt_s[rows, ksl]
        kc = kt_s[rows, ksl]
        khc = kh_s[rows, ksl]
        vc = qkv_ref[rows, vsl]
        kbd = jnp.concatenate([jnp.where(first_head, kc, zero),
                               jnp.where(first_head, zero, kc)], axis=0)
        khbd = jnp.concatenate([jnp.where(first_head, khc, zero),
                                jnp.where(first_head, zero, khc)], axis=0)
        vbd = jnp.concatenate([jnp.where(first_head_v, vc, zero),
                               jnp.where(first_head_v, zero, vc)], axis=0)
        scores = _dot_nt(qc, kbd)
        scores = jnp.where(keep, scores, 0.0).astype(BF16)
        st = st_ref[p]
        o = _dot(scores, vbd) + _dot_nt(qc, st.astype(BF16))
        st_ref[p] = st * dec[:, ksl] + _dot_tn(vbd, khbd)
        o_ref[rows, osl] = o.astype(o_ref.dtype)


def _gla_kernel(qkv_f_ref, la_f_ref, qkv_b_ref, la_b_ref, tril_ref, triu_ref, of_ref, ob_ref,
                qt_f, kt_f, kh_f, dec_f, st_f, qt_b, kt_b, kh_b, dec_b, st_b):
    nchunk = GLA_BLOCK // GLA_CHUNK

    @pl.when(pl.program_id(1) == 0)
    def _():
        st_f[...] = jnp.zeros_like(st_f)
        st_b[...] = jnp.zeros_like(st_b)

    _gla_prepare(qkv_f_ref, la_f_ref, tril_ref, qt_f, kt_f, kh_f, dec_f, False)
    _gla_prepare(qkv_b_ref, la_b_ref, triu_ref, qt_b, kt_b, kh_b, dec_b, True)

    for c in range(nchunk):
        _gla_chunk(c, qkv_f_ref, of_ref, qt_f, kt_f, kh_f, dec_f, st_f, False)
        _gla_chunk(nchunk - 1 - c, qkv_b_ref, ob_ref, qt_b, kt_b, kh_b, dec_b, st_b, True)


def _gla(qkv, la, tril, triu, batch, seq):
    n = qkv.shape[0]
    nb = seq // GLA_BLOCK
    nchunk = GLA_BLOCK // GLA_CHUNK
    qkv_cols = 2 * GLA_K_TOTAL + GLA_V_TOTAL
    fwd = lambda b, i: (b * nb + i, 0)
    bwd = lambda b, i: (b * nb + nb - 1 - i, 0)
    bwd_la = lambda b, i: (b * nb + nb - 1 - i, 1)
    const = lambda b, i: (0, 0)
    dir_scratch = [pltpu.VMEM((GLA_BLOCK, GLA_K_TOTAL), BF16),
                   pltpu.VMEM((GLA_BLOCK, GLA_K_TOTAL), BF16),
                   pltpu.VMEM((GLA_BLOCK, GLA_K_TOTAL), BF16),
                   pltpu.VMEM((nchunk, 1, GLA_K_TOTAL), F32),
                   pltpu.VMEM((GLA_HEADS // 2, 2 * GLA_DV, LANES), F32)]
    return pl.pallas_call(
        _gla_kernel,
        out_shape=(jax.ShapeDtypeStruct((n, GLA_V_TOTAL), BF16),
                   jax.ShapeDtypeStruct((n, GLA_V_TOTAL), BF16)),
        grid=(batch, nb),
        in_specs=[pl.BlockSpec((GLA_BLOCK, qkv_cols), fwd),
                  pl.BlockSpec((GLA_BLOCK, GLA_K_TOTAL), fwd),
                  pl.BlockSpec((GLA_BLOCK, qkv_cols), bwd),
                  pl.BlockSpec((GLA_BLOCK, GLA_K_TOTAL), bwd_la),
                  pl.BlockSpec(tril.shape, const),
                  pl.BlockSpec(triu.shape, const)],
        out_specs=(pl.BlockSpec((GLA_BLOCK, GLA_V_TOTAL), fwd),
                   pl.BlockSpec((GLA_BLOCK, GLA_V_TOTAL), bwd)),
        scratch_shapes=dir_scratch + dir_scratch,
        compiler_params=_params(("arbitrary", "arbitrary")),
        name="gla",
    )(qkv, la, qkv, la, tril, triu)


def _mix_xattn_kernel(tiles_per_seq, x_ref, hb_ref, prev_ref, next_ref, gate_ref, of_ref, ob_ref,
                      kv_ref, convw_ref, convg_ref, gmat_ref, glag_ref, wout_ref,
                      xag_ref, wxq_ref, wxo_ref, o_ref):
    tm = x_ref.shape[0]
    t = pl.program_id(0) % tiles_per_seq
    has_prev = (t > 0).astype(F32)
    has_next = (t < tiles_per_seq - 1).astype(F32)

    h = hb_ref[:, 0:CONV_WIDTH].astype(F32)
    last = BF16_SUBLANES - 1
    h_before = prev_ref[last:last + 1, :].astype(F32) * has_prev
    h_after = next_ref[0:1, :].astype(F32) * has_next
    rowid = lax.broadcasted_iota(jnp.int32, (tm, CONV_WIDTH), 0)
    h_m1 = jnp.where(rowid == 0, h_before, pltpu.roll(h, 1, 0))
    h_p1 = jnp.where(rowid == tm - 1, h_after, pltpu.roll(h, tm - 1, 0))
    y = convw_ref[0:1, :] * h_m1 + convw_ref[1:2, :] * h + convw_ref[2:3, :] * h_p1
    y = hb_ref[:, CONV_WIDTH:2 * CONV_WIDTH].astype(F32) * y
    ms = _dot((y * y).astype(BF16), gmat_ref[...])
    y = y * lax.rsqrt(ms + EPS) * convg_ref[...]

    o = of_ref[...].astype(F32) + ob_ref[...].astype(F32)
    heads = []
    for hd in range(GLA_HEADS):
        oh = o[:, hd * GLA_DV:(hd + 1) * GLA_DV]
        heads.append(_rms(oh, glag_ref[...]))
    o = jnp.concatenate(heads, axis=1) * gate_ref[...].astype(F32)

    mixed = jnp.concatenate([y.astype(BF16), o.astype(BF16)], axis=1)
    x1 = x_ref[...] + _dot(mixed, wout_ref[...])

    r = _inv_rms(x1) * (XA_HEAD_DIM ** -0.5 * LOG2_E)
    q = (_dot((x1 * xag_ref[...]).astype(BF16), wxq_ref[...]) * r).astype(BF16)
    outs = []
    for hd in range(XA_HEADS):
        sl = slice(hd * XA_HEAD_DIM, (hd + 1) * XA_HEAD_DIM)
        s = _dot_nt(q[:, sl], kv_ref[:, sl])
        s = s - jnp.max(s, axis=-1, keepdims=True)
        e = jnp.exp2(s)
        p = e / jnp.sum(e, axis=-1, keepdims=True)
        vsl = slice(D_MODEL + hd * XA_HEAD_DIM, D_MODEL + (hd + 1) * XA_HEAD_DIM)
        outs.append(_dot(p.astype(BF16), kv_ref[:, vsl]).astype(BF16))
    att = jnp.concatenate(outs, axis=1)
    o_ref[...] = x1 + _dot(att, wxo_ref[...])


def _mix_xattn(x2, hb, gate, o_f, o_b, kv, convw, convg, gmat, glag, wout, xag, wxq, wxo, seq):
    n = x2.shape[0]
    tm = MIX_TILE
    tiles_per_seq = seq // tm
    halo_per_tile = tm // BF16_SUBLANES
    n_halo = n // BF16_SUBLANES
    const = lambda i: (0, 0)
    row = lambda i: (i, 0)
    return pl.pallas_call(
        functools.partial(_mix_xattn_kernel, tiles_per_seq),
        out_shape=jax.ShapeDtypeStruct((n, D_MODEL), F32),
        grid=(n // tm,),
        in_specs=[pl.BlockSpec((tm, D_MODEL), row),
                  pl.BlockSpec((tm, 2 * CONV_WIDTH), row),
                  pl.BlockSpec((BF16_SUBLANES, CONV_WIDTH),
                               lambda i: (jnp.maximum(i * halo_per_tile - 1, 0), 0)),
                  pl.BlockSpec((BF16_SUBLANES, CONV_WIDTH),
                               lambda i: (jnp.minimum((i + 1) * halo_per_tile, n_halo - 1), 0)),
                  pl.BlockSpec((tm, GLA_V_TOTAL), row),
                  pl.BlockSpec((tm, GLA_V_TOTAL), row),
                  pl.BlockSpec((tm, GLA_V_TOTAL), row),
                  pl.BlockSpec((N_MEM, 2 * D_MODEL), lambda i: (i // tiles_per_seq, 0)),
                  pl.BlockSpec(convw.shape, const),
                  pl.BlockSpec(convg.shape, const),
                  pl.BlockSpec(gmat.shape, const),
                  pl.BlockSpec(glag.shape, const),
                  pl.BlockSpec(wout.shape, const),
                  pl.BlockSpec(xag.shape, const),
                  pl.BlockSpec(wxq.shape, const),
                  pl.BlockSpec(wxo.shape, const)],
        out_specs=pl.BlockSpec((tm, D_MODEL), row),
        compiler_params=_params(("arbitrary",)),
        name="mix_xattn",
    )(x2, hb, hb, hb, gate, o_f, o_b, kv, convw, convg, gmat, glag, wout, xag, wxq, wxo)


def _mlp_kernel(x_ref, gain_ref, wu_ref, wd_ref, fgain_ref, o_ref):
    x = x_ref[...]
    r = _inv_rms(x)
    xg = (x * gain_ref[...]).astype(BF16)
    acc = None
    for c in range(D_FF // FF_CHUNK):
        sl = slice(c * FF_CHUNK, (c + 1) * FF_CHUNK)
        u = jnp.maximum(_dot(xg, wu_ref[:, sl]), 0.0)
        part = _dot((u * u).astype(BF16), wd_ref[sl, :])
        acc = part if acc is None else acc + part
    o_ref[...] = _rms(x + acc * (r * r), fgain_ref[...])


def _mlp(x2, gain, wu, wd, fgain):
    n = x2.shape[0]
    const = lambda i: (0, 0)
    row = lambda i: (i, 0)
    resident = lambda shape: pl.BlockSpec(shape, const, pipeline_mode=pl.Buffered(1))
    return pl.pallas_call(
        _mlp_kernel,
        out_shape=jax.ShapeDtypeStruct((n, D_MODEL), F32),
        grid=(n // MLP_TILE,),
        in_specs=[pl.BlockSpec((MLP_TILE, D_MODEL), row),
                  pl.BlockSpec(gain.shape, const),
                  resident(wu.shape),
                  resident(wd.shape),
                  pl.BlockSpec(fgain.shape, const)],
        out_specs=pl.BlockSpec((MLP_TILE, D_MODEL), row),
        compiler_params=_params(("arbitrary",)),
        name="mlp",
    )(x2, gain, wu, wd, fgain)


def _block_diag_ones(n, block):
    i = jnp.arange(n)
    return (i[:, None] // block) == (i[None, :] // block)


def kernel(x, mem, mix_norm, w_in, conv_w, conv_norm, w_af, b_af, w_ab, b_ab, gla_norm, w_out, xa_norm, mem_norm, w_xq, w_xkv, w_xo, mlp_norm, w_up, w_down, final_norm):
    batch, seq, d = x.shape
    assert d == D_MODEL and seq % GLA_BLOCK == 0 and seq % TOKEN_TILE == 0
    assert mix_norm.shape[0] == 1, "single-layer block"
    n = batch * seq
    x2 = x.reshape(n, d)
    row2 = lambda v: v.reshape(1, -1).astype(F32)

    cw = CONV_WIDTH
    wa = jnp.zeros((LANES, 2 * GLA_K_TOTAL), F32)
    wa = wa.at[0:GLA_LOWRANK, 0:GLA_K_TOTAL].set(w_af[0])
    wa = wa.at[GLA_LOWRANK:2 * GLA_LOWRANK, GLA_K_TOTAL:].set(w_ab[0]).astype(BF16)
    ba = jnp.concatenate([b_af[0], b_ab[0]]).reshape(1, -1).astype(F32)
    ci = jnp.arange(CUMSUM_BLOCK)
    same_chunk = _block_diag_ones(CUMSUM_BLOCK, GLA_CHUNK)
    tril = (same_chunk & (ci[None, :] <= ci[:, None])).astype(BF16)
    triu = (same_chunk & (ci[None, :] >= ci[:, None])).astype(BF16)
    gmat = (_block_diag_ones(cw, CONV_GROUP).astype(F32) / CONV_GROUP).astype(BF16)

    hb, qkv, gate, la = _in_proj(x2, row2(mix_norm[0]), w_in[0].T, wa, ba)
    kv = _mem_kv(mem.reshape(batch * N_MEM, d), row2(mem_norm[0]), w_xkv[0].astype(BF16))
    o_f, o_b = _gla(qkv, la, tril, triu, batch, seq)
    xa = _mix_xattn(x2, hb, gate, o_f, o_b, kv, conv_w[0].astype(F32), row2(conv_norm[0]), gmat,
                    row2(gla_norm[0]), w_out[0].astype(BF16), row2(xa_norm[0]),
                    w_xq[0].astype(BF16), w_xo[0].astype(BF16), seq)
    out = _mlp(xa, row2(mlp_norm[0]), w_up[0].astype(BF16), w_down[0].astype(BF16),
               row2(final_norm))
    return out.reshape(batch, seq, d)
```

```python
import functools

import jax
import jax.numpy as jnp
from jax import lax
from jax.experimental import pallas as pl
from jax.experimental.pallas import tpu as pltpu

F32 = jnp.float32
BF16 = jnp.bfloat16

EPS = 1e-6
LOG2_E = 1.4426950408889634
D_MODEL = 1024
CONV_WIDTH = 512
CONV_GROUP = 64
GLA_HEADS = 4
GLA_DK = 64
GLA_DV = 128
GLA_K_TOTAL = GLA_HEADS * GLA_DK
GLA_V_TOTAL = GLA_HEADS * GLA_DV
GLA_LOWRANK = 16
GLA_GATE_NORM = 16.0
GLA_CHUNK = 64
N_MEM = 256
XA_HEADS = 4
XA_HEAD_DIM = D_MODEL // XA_HEADS
D_FF = 4 * D_MODEL

LANES = 128
BF16_SUBLANES = 16
VMEM_LIMIT_BYTES = 56 * 1024 * 1024

TOKEN_TILE = 1024
MIX_TILE = 1024
MLP_TILE = 1024
XA_ROW_SPLIT = 4
WEIGHT_PREP_ROWS = 128
GLA_BLOCK = 1024
CUMSUM_BLOCK = 256
FF_CHUNK = 4096


def _dot(a, b):
    return jnp.dot(a, b, preferred_element_type=F32)


def _dot_nt(a, b):
    return lax.dot_general(a, b, (((1,), (1,)), ((), ())), preferred_element_type=F32)


def _dot_tn(a, b):
    return lax.dot_general(a, b, (((0,), (0,)), ((), ())), preferred_element_type=F32)


def _inv_rms(x):
    return lax.rsqrt(jnp.mean(x * x, axis=-1, keepdims=True) + EPS)


def _rms(x, gain):
    return x * _inv_rms(x) * gain


def _ordered_after(lhs, deps):
    bits = None
    for dep in deps:
        words = pltpu.bitcast(dep, jnp.uint32)
        for r in range(0, words.shape[0], 8):
            for c in range(0, words.shape[1], LANES):
                word = words[r:r + 8, c:c + LANES]
                bits = word if bits is None else bits | word
    zero = pltpu.bitcast((bits >> 16) >> 16, F32)
    zero = jnp.concatenate([zero, zero], axis=0).astype(BF16)
    zero = jnp.concatenate([zero] * (lhs.shape[1] // LANES), axis=1)
    return jnp.concatenate([lhs[0:BF16_SUBLANES] + zero, lhs[BF16_SUBLANES:]], axis=0)


def _params(semantics):
    return pltpu.CompilerParams(dimension_semantics=semantics,
                                vmem_limit_bytes=VMEM_LIMIT_BYTES)


def _in_proj_kernel(x_ref, gain_ref, wt_ref, wa_ref, ba_ref, hb_ref, qkv_ref, gate_ref, la_ref,
                    wc_ref, wg_ref):
    cw = CONV_WIDTH
    o_q = 3 * cw
    o_k = o_q + GLA_K_TOTAL
    o_lr = o_q + 2 * GLA_K_TOTAL + 2 * GLA_V_TOTAL

    @pl.when(pl.program_id(0) == 0)
    def _():
        def convert(dst_ref, dst0, src0, rows, scale=None):
            for r in range(0, rows, WEIGHT_PREP_ROWS):
                nr = min(WEIGHT_PREP_ROWS, rows - r)
                blk = wt_ref[src0 + r:src0 + r + nr, :]
                if scale is not None:
                    blk = blk * scale
                dst_ref[dst0 + r:dst0 + r + nr, :] = blk.astype(BF16)

        convert(wc_ref, 0, cw, 2 * cw)
        convert(wc_ref, 2 * cw, 0, cw)
        wg_ref[0:LANES, :] = jnp.zeros((LANES, D_MODEL), BF16)
        convert(wg_ref, 0, o_lr, 2 * GLA_LOWRANK)
        convert(wg_ref, LANES, o_q, GLA_K_TOTAL, GLA_DK ** -0.5)
        convert(wg_ref, LANES + GLA_K_TOTAL, o_k, o_lr - o_k)

    x = x_ref[...]
    r = _inv_rms(x)
    xg = (x * gain_ref[...]).astype(BF16)
    qkv_cols = 2 * GLA_K_TOTAL + GLA_V_TOTAL
    zc = _dot_nt(xg, wc_ref[...])
    hb_ref[:, 0:cw] = (zc[:, 0:cw] * zc[:, cw:2 * cw] * (r * r)).astype(BF16)
    hb_ref[:, cw:2 * cw] = (zc[:, 2 * cw:3 * cw] * r).astype(BF16)
    zg = _dot_nt(xg, wg_ref[...]) * r
    qkv_ref[...] = zg[:, LANES:LANES + qkv_cols].astype(BF16)
    g = zg[:, LANES + qkv_cols:]
    gate_ref[...] = (g / (1.0 + jnp.exp(-g))).astype(BF16)
    codes = zg[:, 0:LANES].astype(BF16)
    z = _dot(codes, wa_ref[...]) + ba_ref[...]
    l2a = ((jnp.minimum(z, 0.0) - jnp.log(1.0 + jnp.exp(-jnp.abs(z))))
           * (LOG2_E / GLA_GATE_NORM))
    la_ref[...] = l2a.astype(BF16)


def _in_proj(x2, gain, w_in_t, wa, ba):
    n = x2.shape[0]
    const = lambda i: (0, 0)
    row = lambda i: (i, 0)
    conv_cols = 3 * CONV_WIDTH
    gla_cols = 2 * GLA_K_TOTAL + 2 * GLA_V_TOTAL
    assert w_in_t.shape == (conv_cols + gla_cols + 2 * GLA_LOWRANK, D_MODEL)
    outs = (2 * CONV_WIDTH, 2 * GLA_K_TOTAL + GLA_V_TOTAL, GLA_V_TOTAL, 2 * GLA_K_TOTAL)
    return pl.pallas_call(
        _in_proj_kernel,
        out_shape=tuple(jax.ShapeDtypeStruct((n, c), BF16) for c in outs),
        grid=(n // TOKEN_TILE,),
        in_specs=[pl.BlockSpec((TOKEN_TILE, D_MODEL), row),
                  pl.BlockSpec(gain.shape, const),
                  pl.BlockSpec(w_in_t.shape, const, pipeline_mode=pl.Buffered(1)),
                  pl.BlockSpec(wa.shape, const),
                  pl.BlockSpec(ba.shape, const)],
        out_specs=tuple(pl.BlockSpec((TOKEN_TILE, c), row) for c in outs),
        scratch_shapes=[pltpu.VMEM((c, D_MODEL), BF16) for c in (conv_cols, LANES + gla_cols)],
        compiler_params=_params(("arbitrary",)),
        name="in_proj",
    )(x2, gain, w_in_t, wa, ba)


def _mem_kv_kernel(m_ref, gain_ref, w_ref, kt_ref, v_ref):
    h = _rms(m_ref[...], gain_ref[...]).astype(BF16)
    kv = _dot(h, w_ref[...])
    kt_ref[...] = kv[:, 0:D_MODEL].T.astype(BF16)
    v_ref[...] = kv[:, D_MODEL:].astype(BF16)


def _mem_kv(mem2, gain, w):
    n = mem2.shape[0]
    batch = n // N_MEM
    const = lambda i: (0, 0)
    row = lambda i: (i, 0)
    return pl.pallas_call(
        _mem_kv_kernel,
        out_shape=(jax.ShapeDtypeStruct((batch * D_MODEL, N_MEM), BF16),
                   jax.ShapeDtypeStruct((n, D_MODEL), BF16)),
        grid=(batch,),
        in_specs=[pl.BlockSpec((N_MEM, D_MODEL), row),
                  pl.BlockSpec(gain.shape, const),
                  pl.BlockSpec(w.shape, const)],
        out_specs=(pl.BlockSpec((D_MODEL, N_MEM), row),
                   pl.BlockSpec((N_MEM, D_MODEL), row)),
        compiler_params=_params(("arbitrary",)),
        name="mem_kv",
    )(mem2, gain, w)


def _gla_prepare(qkv_ref, la_ref, tri_ref, qt_s, kt_s, kh_s, dec_s, reverse):
    tri = tri_ref[...]
    for r in range(GLA_BLOCK // CUMSUM_BLOCK):
        b = _dot(tri, la_ref[r * CUMSUM_BLOCK:(r + 1) * CUMSUM_BLOCK, :])
        grow = jnp.exp2(b)
        shrink = jnp.exp2(-b)
        rows = slice(r * CUMSUM_BLOCK, (r + 1) * CUMSUM_BLOCK)
        qt_s[rows, :] = qkv_ref[rows, 0:GLA_K_TOTAL] * grow.astype(BF16)
        kt = qkv_ref[rows, GLA_K_TOTAL:2 * GLA_K_TOTAL] * shrink.astype(BF16)
        kt_s[rows, :] = kt
        for cc in range(CUMSUM_BLOCK // GLA_CHUNK):
            c = r * (CUMSUM_BLOCK // GLA_CHUNK) + cc
            last = cc * GLA_CHUNK + (0 if reverse else GLA_CHUNK - 1)
            dec = grow[last:last + 1, :]
            dec_s[c] = dec
            crow = slice(cc * GLA_CHUNK, (cc + 1) * GLA_CHUNK)
            kh_s[c * GLA_CHUNK:(c + 1) * GLA_CHUNK, :] = kt[crow] * dec.astype(BF16)


def _gla_chunk(c, qkv_ref, o_ref, qt_s, kt_s, kh_s, dec_s, st_ref, reverse):
    rows = slice(c * GLA_CHUNK, (c + 1) * GLA_CHUNK)
    lane = lax.broadcasted_iota(jnp.int32, (GLA_CHUNK, LANES), 1)
    row = lax.broadcasted_iota(jnp.int32, (GLA_CHUNK, LANES), 0)
    first_head = lane < GLA_DK
    src = lane & (GLA_DK - 1)
    keep = (src > row) if reverse else (src <= row)
    vlane = lax.broadcasted_iota(jnp.int32, (GLA_CHUNK, 2 * GLA_DV), 1)
    first_head_v = vlane < GLA_DV
    dec = dec_s[c]
    zero = jnp.zeros((), BF16)
    for p in range(GLA_HEADS // 2):
        ksl = slice(p * LANES, (p + 1) * LANES)
        osl = slice(p * 2 * GLA_DV, (p + 1) * 2 * GLA_DV)
        vsl = slice(2 * GLA_K_TOTAL + p * 2 * GLA_DV, 2 * GLA_K_TOTAL + (p + 1) * 2 * GLA_DV)
        qc = qt_s[rows, ksl]
        kc = kt_s[rows, ksl]
        khc = kh_s[rows, ksl]
        vc = qkv_ref[rows, vsl]
        kbd = jnp.concatenate([jnp.where(first_head, kc, zero),
                               jnp.where(first_head, zero, kc)], axis=0)
        khbd = jnp.concatenate([jnp.where(first_head, khc, zero),
                                jnp.where(first_head, zero, khc)], axis=0)
        vbd = jnp.concatenate([jnp.where(first_head_v, vc, zero),
                               jnp.where(first_head_v, zero, vc)], axis=0)
        scores = _dot_nt(qc, kbd)
        scores = jnp.where(keep, scores, 0.0).astype(BF16)
        st = st_ref[p]
        o = _dot(scores, vbd) + _dot_nt(qc, st.astype(BF16))
        st_ref[p] = st * dec[:, ksl] + _dot_tn(vbd, khbd)
        o_ref[rows, osl] = o.astype(o_ref.dtype)


def _gla_kernel(qkv_f_ref, la_f_ref, qkv_b_ref, la_b_ref, tril_ref, triu_ref, of_ref, ob_ref,
                qt_f, kt_f, kh_f, dec_f, st_f, qt_b, kt_b, kh_b, dec_b, st_b):
    nchunk = GLA_BLOCK // GLA_CHUNK

    @pl.when(pl.program_id(1) == 0)
    def _():
        st_f[...] = jnp.zeros_like(st_f)
        st_b[...] = jnp.zeros_like(st_b)

    _gla_prepare(qkv_f_ref, la_f_ref, tril_ref, qt_f, kt_f, kh_f, dec_f, False)
    _gla_prepare(qkv_b_ref, la_b_ref, triu_ref, qt_b, kt_b, kh_b, dec_b, True)

    for c in range(nchunk):
        _gla_chunk(c, qkv_f_ref, of_ref, qt_f, kt_f, kh_f, dec_f, st_f, False)
        _gla_chunk(nchunk - 1 - c, qkv_b_ref, ob_ref, qt_b, kt_b, kh_b, dec_b, st_b, True)


def _gla(qkv, la, tril, triu, batch, seq):
    n = qkv.shape[0]
    nb = seq // GLA_BLOCK
    nchunk = GLA_BLOCK // GLA_CHUNK
    qkv_cols = 2 * GLA_K_TOTAL + GLA_V_TOTAL
    fwd = lambda b, i: (b * nb + i, 0)
    bwd = lambda b, i: (b * nb + nb - 1 - i, 0)
    bwd_la = lambda b, i: (b * nb + nb - 1 - i, 1)
    const = lambda b, i: (0, 0)
    dir_scratch = [pltpu.VMEM((GLA_BLOCK, GLA_K_TOTAL), BF16),
                   pltpu.VMEM((GLA_BLOCK, GLA_K_TOTAL), BF16),
                   pltpu.VMEM((GLA_BLOCK, GLA_K_TOTAL), BF16),
                   pltpu.VMEM((nchunk, 1, GLA_K_TOTAL), F32),
                   pltpu.VMEM((GLA_HEADS // 2, 2 * GLA_DV, LANES), F32)]
    return pl.pallas_call(
        _gla_kernel,
        out_shape=(jax.ShapeDtypeStruct((n, GLA_V_TOTAL), BF16),
                   jax.ShapeDtypeStruct((n, GLA_V_TOTAL), BF16)),
        grid=(batch, nb),
        in_specs=[pl.BlockSpec((GLA_BLOCK, qkv_cols), fwd),
                  pl.BlockSpec((GLA_BLOCK, GLA_K_TOTAL), fwd),
                  pl.BlockSpec((GLA_BLOCK, qkv_cols), bwd),
                  pl.BlockSpec((GLA_BLOCK, GLA_K_TOTAL), bwd_la),
                  pl.BlockSpec(tril.shape, const),
                  pl.BlockSpec(triu.shape, const)],
        out_specs=(pl.BlockSpec((GLA_BLOCK, GLA_V_TOTAL), fwd),
                   pl.BlockSpec((GLA_BLOCK, GLA_V_TOTAL), bwd)),
        scratch_shapes=dir_scratch + dir_scratch,
        compiler_params=_params(("arbitrary", "arbitrary")),
        name="gla",
    )(qkv, la, qkv, la, tril, triu)


def _mix_xattn_kernel(tiles_per_seq, x_ref, hb_ref, prev_ref, next_ref, gate_ref, of_ref, ob_ref,
                      kt_ref, vm_ref, convw_ref, convg_ref, gmat_ref, glag_ref, wout_ref,
                      xag_ref, wxq_ref, wxo_ref, o_ref):
    tm = x_ref.shape[0]
    t = pl.program_id(0) % tiles_per_seq
    has_prev = (t > 0).astype(F32)
    has_next = (t < tiles_per_seq - 1).astype(F32)

    h = hb_ref[:, 0:CONV_WIDTH].astype(F32)
    last = BF16_SUBLANES - 1
    h_before = prev_ref[last:last + 1, :].astype(F32) * has_prev
    h_after = next_ref[0:1, :].astype(F32) * has_next
    rowid = lax.broadcasted_iota(jnp.int32, (tm, CONV_WIDTH), 0)
    h_m1 = jnp.where(rowid == 0, h_before, pltpu.roll(h, 1, 0))
    h_p1 = jnp.where(rowid == tm - 1, h_after, pltpu.roll(h, tm - 1, 0))
    y = convw_ref[0:1, :] * h_m1 + convw_ref[1:2, :] * h + convw_ref[2:3, :] * h_p1
    y = hb_ref[:, CONV_WIDTH:2 * CONV_WIDTH].astype(F32) * y
    ms = _dot((y * y).astype(BF16), gmat_ref[...])
    y = y * lax.rsqrt(ms + EPS) * convg_ref[...]

    o = of_ref[...].astype(F32) + ob_ref[...].astype(F32)
    heads = []
    for hd in range(GLA_HEADS):
        oh = o[:, hd * GLA_DV:(hd + 1) * GLA_DV]
        heads.append(_rms(oh, glag_ref[...]))
    o = jnp.concatenate(heads, axis=1) * gate_ref[...].astype(F32)

    mixed = jnp.concatenate([y.astype(BF16), o.astype(BF16)], axis=1)
    x1 = x_ref[...] + _dot(mixed, wout_ref[...])

    r = _inv_rms(x1) * (XA_HEAD_DIM ** -0.5 * LOG2_E)
    q = (_dot((x1 * xag_ref[...]).astype(BF16), wxq_ref[...]) * r).astype(BF16)
    half_rows = tm // XA_ROW_SPLIT
    halves = []
    for top in range(0, tm, half_rows):
        rows = slice(top, top + half_rows)
        outs = []
        for hd in range(XA_HEADS):
            sl = slice(hd * XA_HEAD_DIM, (hd + 1) * XA_HEAD_DIM)
            s = _dot(q[rows, sl], kt_ref[sl, :])
            s = s - jnp.max(s, axis=-1, keepdims=True)
            e = jnp.exp2(s)
            p = e / jnp.sum(e, axis=-1, keepdims=True)
            outs.append(_dot(p.astype(BF16), vm_ref[:, sl]).astype(BF16))
        halves.append(jnp.concatenate(outs, axis=1))
    att = jnp.concatenate(halves, axis=0)
    o_ref[...] = x1 + _dot(att, wxo_ref[...])


def _mix_xattn(x2, hb, gate, o_f, o_b, kt, vm, convw, convg, gmat, glag, wout, xag, wxq, wxo, seq):
    n = x2.shape[0]
    tm = MIX_TILE
    tiles_per_seq = seq // tm
    halo_per_tile = tm // BF16_SUBLANES
    n_halo = n // BF16_SUBLANES
    const = lambda i: (0, 0)
    row = lambda i: (i, 0)
    return pl.pallas_call(
        functools.partial(_mix_xattn_kernel, tiles_per_seq),
        out_shape=jax.ShapeDtypeStruct((n, D_MODEL), F32),
        grid=(n // tm,),
        in_specs=[pl.BlockSpec((tm, D_MODEL), row),
                  pl.BlockSpec((tm, 2 * CONV_WIDTH), row),
                  pl.BlockSpec((BF16_SUBLANES, CONV_WIDTH),
                               lambda i: (jnp.maximum(i * halo_per_tile - 1, 0), 0)),
                  pl.BlockSpec((BF16_SUBLANES, CONV_WIDTH),
                               lambda i: (jnp.minimum((i + 1) * halo_per_tile, n_halo - 1), 0)),
                  pl.BlockSpec((tm, GLA_V_TOTAL), row),
                  pl.BlockSpec((tm, GLA_V_TOTAL), row),
                  pl.BlockSpec((tm, GLA_V_TOTAL), row),
                  pl.BlockSpec((D_MODEL, N_MEM), lambda i: (i // tiles_per_seq, 0)),
                  pl.BlockSpec((N_MEM, D_MODEL), lambda i: (i // tiles_per_seq, 0)),
                  pl.BlockSpec(convw.shape, const),
                  pl.BlockSpec(convg.shape, const),
                  pl.BlockSpec(gmat.shape, const),
                  pl.BlockSpec(glag.shape, const),
                  pl.BlockSpec(wout.shape, const),
                  pl.BlockSpec(xag.shape, const),
                  pl.BlockSpec(wxq.shape, const),
                  pl.BlockSpec(wxo.shape, const)],
        out_specs=pl.BlockSpec((tm, D_MODEL), row),
        compiler_params=_params(("arbitrary",)),
        name="mix_xattn",
    )(x2, hb, hb, hb, gate, o_f, o_b, kt, vm, convw, convg, gmat, glag, wout, xag, wxq, wxo)


def _mlp_kernel(x_ref, gain_ref, wu_ref, wd_ref, fgain_ref, o_ref):
    x = x_ref[...]
    r = _inv_rms(x)
    xg = (x * gain_ref[...]).astype(BF16)
    acc = None
    for c in range(D_FF // FF_CHUNK):
        sl = slice(c * FF_CHUNK, (c + 1) * FF_CHUNK)
        u = jnp.maximum(_dot(xg, wu_ref[:, sl]), 0.0)
        part = _dot((u * u).astype(BF16), wd_ref[sl, :])
        acc = part if acc is None else acc + part
    o_ref[...] = _rms(x + acc * (r * r), fgain_ref[...])


def _mlp(x2, gain, wu, wd, fgain):
    n = x2.shape[0]
    const = lambda i: (0, 0)
    row = lambda i: (i, 0)
    resident = lambda shape: pl.BlockSpec(shape, const, pipeline_mode=pl.Buffered(1))
    return pl.pallas_call(
        _mlp_kernel,
        out_shape=jax.ShapeDtypeStruct((n, D_MODEL), F32),
        grid=(n // MLP_TILE,),
        in_specs=[pl.BlockSpec((MLP_TILE, D_MODEL), row),
                  pl.BlockSpec(gain.shape, const),
                  resident(wu.shape),
                  resident(wd.shape),
                  pl.BlockSpec(fgain.shape, const)],
        out_specs=pl.BlockSpec((MLP_TILE, D_MODEL), row),
        compiler_params=_params(("arbitrary",)),
        name="mlp",
    )(x2, gain, wu, wd, fgain)


def _block_diag_ones(n, block):
    i = jnp.arange(n)
    return (i[:, None] // block) == (i[None, :] // block)


def kernel(x, mem, mix_norm, w_in, conv_w, conv_norm, w_af, b_af, w_ab, b_ab, gla_norm, w_out, xa_norm, mem_norm, w_xq, w_xkv, w_xo, mlp_norm, w_up, w_down, final_norm):
    batch, seq, d = x.shape
    assert d == D_MODEL and seq % GLA_BLOCK == 0 and seq % TOKEN_TILE == 0
    assert mix_norm.shape[0] == 1, "single-layer block"
    n = batch * seq
    x2 = x.reshape(n, d)
    row2 = lambda v: v.reshape(1, -1).astype(F32)

    cw = CONV_WIDTH
    wa = jnp.zeros((LANES, 2 * GLA_K_TOTAL), F32)
    wa = wa.at[0:GLA_LOWRANK, 0:GLA_K_TOTAL].set(w_af[0])
    wa = wa.at[GLA_LOWRANK:2 * GLA_LOWRANK, GLA_K_TOTAL:].set(w_ab[0]).astype(BF16)
    ba = jnp.concatenate([b_af[0], b_ab[0]]).reshape(1, -1).astype(F32)
    ci = jnp.arange(CUMSUM_BLOCK)
    same_chunk = _block_diag_ones(CUMSUM_BLOCK, GLA_CHUNK)
    tril = (same_chunk & (ci[None, :] <= ci[:, None])).astype(BF16)
    triu = (same_chunk & (ci[None, :] >= ci[:, None])).astype(BF16)
    gmat = (_block_diag_ones(cw, CONV_GROUP).astype(F32) / CONV_GROUP).astype(BF16)

    hb, qkv, gate, la = _in_proj(x2, row2(mix_norm[0]), w_in[0].T, wa, ba)
    kt, vm = _mem_kv(mem.reshape(batch * N_MEM, d), row2(mem_norm[0]), w_xkv[0].astype(BF16))
    o_f, o_b = _gla(qkv, la, tril, triu, batch, seq)
    xa = _mix_xattn(x2, hb, gate, o_f, o_b, kt, vm, conv_w[0].astype(F32), row2(conv_norm[0]), gmat,
                    row2(gla_norm[0]), w_out[0].astype(BF16), row2(xa_norm[0]),
                    w_xq[0].astype(BF16), w_xo[0].astype(BF16), seq)
    out = _mlp(xa, row2(mlp_norm[0]), w_up[0].astype(BF16), w_down[0].astype(BF16),
               row2(final_norm))
    return out.reshape(batch, seq, d)
```

```python
import functools

import jax
import jax.numpy as jnp
from jax import lax
from jax.experimental import pallas as pl
from jax.experimental.pallas import tpu as pltpu

F32 = jnp.float32
BF16 = jnp.bfloat16

EPS = 1e-6
LOG2_E = 1.4426950408889634
D_MODEL = 1024
CONV_WIDTH = 512
CONV_GROUP = 64
GLA_HEADS = 4
GLA_DK = 64
GLA_DV = 128
GLA_K_TOTAL = GLA_HEADS * GLA_DK
GLA_V_TOTAL = GLA_HEADS * GLA_DV
GLA_LOWRANK = 16
GLA_GATE_NORM = 16.0
GLA_CHUNK = 64
N_MEM = 256
XA_HEADS = 4
XA_HEAD_DIM = D_MODEL // XA_HEADS
D_FF = 4 * D_MODEL

LANES = 128
BF16_SUBLANES = 16
VMEM_LIMIT_BYTES = 56 * 1024 * 1024

TOKEN_TILE = 1024
MIX_TILE = 1024
MLP_TILE = 1024
XA_ROW_SPLIT = 4
WEIGHT_PREP_ROWS = 128
GLA_BLOCK = 1024
CUMSUM_BLOCK = 256
FF_CHUNK = 4096


def _dot(a, b):
    return jnp.dot(a, b, preferred_element_type=F32)


def _dot_nt(a, b):
    return lax.dot_general(a, b, (((1,), (1,)), ((), ())), preferred_element_type=F32)


def _dot_tn(a, b):
    return lax.dot_general(a, b, (((0,), (0,)), ((), ())), preferred_element_type=F32)


def _inv_rms(x):
    return lax.rsqrt(jnp.mean(x * x, axis=-1, keepdims=True) + EPS)


def _rms(x, gain):
    return x * _inv_rms(x) * gain


def _ordered_after(lhs, deps):
    bits = None
    for dep in deps:
        words = pltpu.bitcast(dep, jnp.uint32)
        for r in range(0, words.shape[0], 8):
            for c in range(0, words.shape[1], LANES):
                word = words[r:r + 8, c:c + LANES]
                bits = word if bits is None else bits | word
    zero = pltpu.bitcast((bits >> 16) >> 16, F32)
    zero = jnp.concatenate([zero, zero], axis=0).astype(BF16)
    zero = jnp.concatenate([zero] * (lhs.shape[1] // LANES), axis=1)
    return jnp.concatenate([lhs[0:BF16_SUBLANES] + zero, lhs[BF16_SUBLANES:]], axis=0)


def _params(semantics):
    return pltpu.CompilerParams(dimension_semantics=semantics,
                                vmem_limit_bytes=VMEM_LIMIT_BYTES)


def _in_proj_kernel(x_ref, gain_ref, wt_ref, wa_ref, ba_ref, hb_ref, qkv_ref, gate_ref, la_ref,
                    wc_ref, wg_ref):
    cw = CONV_WIDTH
    o_q = 3 * cw
    o_k = o_q + GLA_K_TOTAL
    o_lr = o_q + 2 * GLA_K_TOTAL + 2 * GLA_V_TOTAL

    @pl.when(pl.program_id(0) == 0)
    def _():
        def convert(dst_ref, dst0, src0, rows, scale=None):
            for r in range(0, rows, WEIGHT_PREP_ROWS):
                nr = min(WEIGHT_PREP_ROWS, rows - r)
                blk = wt_ref[src0 + r:src0 + r + nr, :]
                if scale is not None:
                    blk = blk * scale
                dst_ref[dst0 + r:dst0 + r + nr, :] = blk.astype(BF16)

        convert(wc_ref, 0, cw, 2 * cw)
        convert(wc_ref, 2 * cw, 0, cw)
        wg_ref[0:LANES, :] = jnp.zeros((LANES, D_MODEL), BF16)
        convert(wg_ref, 0, o_lr, 2 * GLA_LOWRANK)
        convert(wg_ref, LANES, o_q, GLA_K_TOTAL, GLA_DK ** -0.5)
        convert(wg_ref, LANES + GLA_K_TOTAL, o_k, o_lr - o_k)

    x = x_ref[...]
    r = _inv_rms(x)
    xg = (x * gain_ref[...]).astype(BF16)
    qkv_cols = 2 * GLA_K_TOTAL + GLA_V_TOTAL
    zg = _dot_nt(xg, wg_ref[...]) * r
    qkv_ref[...] = zg[:, LANES:LANES + qkv_cols].astype(BF16)
    g = zg[:, LANES + qkv_cols:]
    gate_ref[...] = (g / (1.0 + jnp.exp(-g))).astype(BF16)
    codes = zg[:, 0:LANES].astype(BF16)
    z = _dot(codes, wa_ref[...]) + ba_ref[...]
    l2a = ((jnp.minimum(z, 0.0) - jnp.log(1.0 + jnp.exp(-jnp.abs(z))))
           * (LOG2_E / GLA_GATE_NORM))
    la_ref[...] = l2a.astype(BF16)
    zc = _dot_nt(xg, wc_ref[...])
    hb_ref[:, 0:cw] = (zc[:, 0:cw] * zc[:, cw:2 * cw] * (r * r)).astype(BF16)
    hb_ref[:, cw:2 * cw] = (zc[:, 2 * cw:3 * cw] * r).astype(BF16)


def _in_proj(x2, gain, w_in_t, wa, ba):
    n = x2.shape[0]
    const = lambda i: (0, 0)
    row = lambda i: (i, 0)
    conv_cols = 3 * CONV_WIDTH
    gla_cols = 2 * GLA_K_TOTAL + 2 * GLA_V_TOTAL
    assert w_in_t.shape == (conv_cols + gla_cols + 2 * GLA_LOWRANK, D_MODEL)
    outs = (2 * CONV_WIDTH, 2 * GLA_K_TOTAL + GLA_V_TOTAL, GLA_V_TOTAL, 2 * GLA_K_TOTAL)
    return pl.pallas_call(
        _in_proj_kernel,
        out_shape=tuple(jax.ShapeDtypeStruct((n, c), BF16) for c in outs),
        grid=(n // TOKEN_TILE,),
        in_specs=[pl.BlockSpec((TOKEN_TILE, D_MODEL), row),
                  pl.BlockSpec(gain.shape, const),
                  pl.BlockSpec(w_in_t.shape, const, pipeline_mode=pl.Buffered(1)),
                  pl.BlockSpec(wa.shape, const),
                  pl.BlockSpec(ba.shape, const)],
        out_specs=tuple(pl.BlockSpec((TOKEN_TILE, c), row) for c in outs),
        scratch_shapes=[pltpu.VMEM((c, D_MODEL), BF16) for c in (conv_cols, LANES + gla_cols)],
        compiler_params=_params(("arbitrary",)),
        name="in_proj",
    )(x2, gain, w_in_t, wa, ba)


def _mem_kv_kernel(m_ref, gain_ref, w_ref, kt_ref, v_ref):
    h = _rms(m_ref[...], gain_ref[...]).astype(BF16)
    kv = _dot(h, w_ref[...])
    v_ref[...] = kv[:, D_MODEL:].astype(BF16)
    for b in range(m_ref.shape[0] // N_MEM):
        keys = kv[b * N_MEM:(b + 1) * N_MEM, 0:D_MODEL]
        kt_ref[b * D_MODEL:(b + 1) * D_MODEL, :] = keys.T.astype(BF16)


def _mem_kv(mem2, gain, w):
    n = mem2.shape[0]
    batch = n // N_MEM
    const = lambda i: (0, 0)
    return pl.pallas_call(
        _mem_kv_kernel,
        out_shape=(jax.ShapeDtypeStruct((batch * D_MODEL, N_MEM), BF16),
                   jax.ShapeDtypeStruct((n, D_MODEL), BF16)),
        grid=(1,),
        in_specs=[pl.BlockSpec(mem2.shape, const),
                  pl.BlockSpec(gain.shape, const),
                  pl.BlockSpec(w.shape, const)],
        out_specs=(pl.BlockSpec((batch * D_MODEL, N_MEM), const),
                   pl.BlockSpec((n, D_MODEL), const)),
        compiler_params=_params(("arbitrary",)),
        name="mem_kv",
    )(mem2, gain, w)


def _gla_prepare(qkv_ref, la_ref, tri_ref, qt_s, kt_s, kh_s, dec_s, reverse):
    tri = tri_ref[...]
    for r in range(GLA_BLOCK // CUMSUM_BLOCK):
        b = _dot(tri, la_ref[r * CUMSUM_BLOCK:(r + 1) * CUMSUM_BLOCK, :])
        grow = jnp.exp2(b)
        shrink = jnp.exp2(-b)
        rows = slice(r * CUMSUM_BLOCK, (r + 1) * CUMSUM_BLOCK)
        qt_s[rows, :] = qkv_ref[rows, 0:GLA_K_TOTAL] * grow.astype(BF16)
        kt = qkv_ref[rows, GLA_K_TOTAL:2 * GLA_K_TOTAL] * shrink.astype(BF16)
        kt_s[rows, :] = kt
        for cc in range(CUMSUM_BLOCK // GLA_CHUNK):
            c = r * (CUMSUM_BLOCK // GLA_CHUNK) + cc
            last = cc * GLA_CHUNK + (0 if reverse else GLA_CHUNK - 1)
            dec = grow[last:last + 1, :]
            dec_s[c] = dec
            crow = slice(cc * GLA_CHUNK, (cc + 1) * GLA_CHUNK)
            kh_s[c * GLA_CHUNK:(c + 1) * GLA_CHUNK, :] = kt[crow] * dec.astype(BF16)


def _gla_chunk(c, qkv_ref, o_ref, qt_s, kt_s, kh_s, dec_s, st_ref, reverse):
    rows = slice(c * GLA_CHUNK, (c + 1) * GLA_CHUNK)
    lane = lax.broadcasted_iota(jnp.int32, (GLA_CHUNK, LANES), 1)
    row = lax.broadcasted_iota(jnp.int32, (GLA_CHUNK, LANES), 0)
    first_head = lane < GLA_DK
    src = lane & (GLA_DK - 1)
    keep = (src > row) if reverse else (src <= row)
    vlane = lax.broadcasted_iota(jnp.int32, (GLA_CHUNK, 2 * GLA_DV), 1)
    first_head_v = vlane < GLA_DV
    dec = dec_s[c]
    zero = jnp.zeros((), BF16)
    for p in range(GLA_HEADS // 2):
        ksl = slice(p * LANES, (p + 1) * LANES)
        osl = slice(p * 2 * GLA_DV, (p + 1) * 2 * GLA_DV)
        vsl = slice(2 * GLA_K_TOTAL + p * 2 * GLA_DV, 2 * GLA_K_TOTAL + (p + 1) * 2 * GLA_DV)
        qc = qt_s[rows, ksl]
        kc = kt_s[rows, ksl]
        khc = kh_s[rows, ksl]
        vc = qkv_ref[rows, vsl]
        kbd = jnp.concatenate([jnp.where(first_head, kc, zero),
                               jnp.where(first_head, zero, kc)], axis=0)
        khbd = jnp.concatenate([jnp.where(first_head, khc, zero),
                                jnp.where(first_head, zero, khc)], axis=0)
        vbd = jnp.concatenate([jnp.where(first_head_v, vc, zero),
                               jnp.where(first_head_v, zero, vc)], axis=0)
        scores = _dot_nt(qc, kbd)
        scores = jnp.where(keep, scores, 0.0).astype(BF16)
        st = st_ref[p]
        o = _dot(scores, vbd) + _dot_nt(qc, st.astype(BF16))
        st_ref[p] = st * dec[:, ksl] + _dot_tn(vbd, khbd)
        o_ref[rows, osl] = o.astype(o_ref.dtype)


def _gla_kernel(qkv_f_ref, la_f_ref, qkv_b_ref, la_b_ref, tril_ref, triu_ref, of_ref, ob_ref,
                qt_f, kt_f, kh_f, dec_f, st_f, qt_b, kt_b, kh_b, dec_b, st_b):
    nchunk = GLA_BLOCK // GLA_CHUNK

    @pl.when(pl.program_id(1) == 0)
    def _():
        st_f[...] = jnp.zeros_like(st_f)
        st_b[...] = jnp.zeros_like(st_b)

    _gla_prepare(qkv_f_ref, la_f_ref, tril_ref, qt_f, kt_f, kh_f, dec_f, False)
    _gla_prepare(qkv_b_ref, la_b_ref, triu_ref, qt_b, kt_b, kh_b, dec_b, True)

    for c in range(nchunk):
        _gla_chunk(c, qkv_f_ref, of_ref, qt_f, kt_f, kh_f, dec_f, st_f, False)
        _gla_chunk(nchunk - 1 - c, qkv_b_ref, ob_ref, qt_b, kt_b, kh_b, dec_b, st_b, True)


def _gla(qkv, la, tril, triu, batch, seq):
    n = qkv.shape[0]
    nb = seq // GLA_BLOCK
    nchunk = GLA_BLOCK // GLA_CHUNK
    qkv_cols = 2 * GLA_K_TOTAL + GLA_V_TOTAL
    fwd = lambda b, i: (b * nb + i, 0)
    bwd = lambda b, i: (b * nb + nb - 1 - i, 0)
    bwd_la = lambda b, i: (b * nb + nb - 1 - i, 1)
    const = lambda b, i: (0, 0)
    dir_scratch = [pltpu.VMEM((GLA_BLOCK, GLA_K_TOTAL), BF16),
                   pltpu.VMEM((GLA_BLOCK, GLA_K_TOTAL), BF16),
                   pltpu.VMEM((GLA_BLOCK, GLA_K_TOTAL), BF16),
                   pltpu.VMEM((nchunk, 1, GLA_K_TOTAL), F32),
                   pltpu.VMEM((GLA_HEADS // 2, 2 * GLA_DV, LANES), F32)]
    return pl.pallas_call(
        _gla_kernel,
        out_shape=(jax.ShapeDtypeStruct((n, GLA_V_TOTAL), BF16),
                   jax.ShapeDtypeStruct((n, GLA_V_TOTAL), BF16)),
        grid=(batch, nb),
        in_specs=[pl.BlockSpec((GLA_BLOCK, qkv_cols), fwd),
                  pl.BlockSpec((GLA_BLOCK, GLA_K_TOTAL), fwd),
                  pl.BlockSpec((GLA_BLOCK, qkv_cols), bwd),
                  pl.BlockSpec((GLA_BLOCK, GLA_K_TOTAL), bwd_la),
                  pl.BlockSpec(tril.shape, const),
                  pl.BlockSpec(triu.shape, const)],
        out_specs=(pl.BlockSpec((GLA_BLOCK, GLA_V_TOTAL), fwd),
                   pl.BlockSpec((GLA_BLOCK, GLA_V_TOTAL), bwd)),
        scratch_shapes=dir_scratch + dir_scratch,
        compiler_params=_params(("arbitrary", "arbitrary")),
        name="gla",
    )(qkv, la, qkv, la, tril, triu)


def _mix_xattn_kernel(tiles_per_seq, x_ref, hb_ref, prev_ref, next_ref, gate_ref, of_ref, ob_ref,
                      kt_ref, vm_ref, convw_ref, convg_ref, gmat_ref, glag_ref, wout_ref,
                      xag_ref, wxq_ref, wxo_ref, o_ref):
    tm = x_ref.shape[0]
    t = pl.program_id(0) % tiles_per_seq
    has_prev = (t > 0).astype(F32)
    has_next = (t < tiles_per_seq - 1).astype(F32)

    h = hb_ref[:, 0:CONV_WIDTH].astype(F32)
    last = BF16_SUBLANES - 1
    h_before = prev_ref[last:last + 1, :].astype(F32) * has_prev
    h_after = next_ref[0:1, :].astype(F32) * has_next
    rowid = lax.broadcasted_iota(jnp.int32, (tm, CONV_WIDTH), 0)
    h_m1 = jnp.where(rowid == 0, h_before, pltpu.roll(h, 1, 0))
    h_p1 = jnp.where(rowid == tm - 1, h_after, pltpu.roll(h, tm - 1, 0))
    y = convw_ref[0:1, :] * h_m1 + convw_ref[1:2, :] * h + convw_ref[2:3, :] * h_p1
    y = hb_ref[:, CONV_WIDTH:2 * CONV_WIDTH].astype(F32) * y
    ms = _dot((y * y).astype(BF16), gmat_ref[...])
    y = y * lax.rsqrt(ms + EPS) * convg_ref[...]

    o = of_ref[...].astype(F32) + ob_ref[...].astype(F32)
    heads = []
    for hd in range(GLA_HEADS):
        oh = o[:, hd * GLA_DV:(hd + 1) * GLA_DV]
        heads.append(_rms(oh, glag_ref[...]))
    o = jnp.concatenate(heads, axis=1) * gate_ref[...].astype(F32)

    mixed = jnp.concatenate([y.astype(BF16), o.astype(BF16)], axis=1)
    x1 = x_ref[...] + _dot(mixed, wout_ref[...])

    r = _inv_rms(x1) * (XA_HEAD_DIM ** -0.5 * LOG2_E)
    q = (_dot((x1 * xag_ref[...]).astype(BF16), wxq_ref[...]) * r).astype(BF16)
    half_rows = tm // XA_ROW_SPLIT
    halves = []
    for top in range(0, tm, half_rows):
        rows = slice(top, top + half_rows)
        outs = []
        for hd in range(XA_HEADS):
            sl = slice(hd * XA_HEAD_DIM, (hd + 1) * XA_HEAD_DIM)
            s = _dot(q[rows, sl], kt_ref[sl, :])
            s = s - jnp.max(s, axis=-1, keepdims=True)
            e = jnp.exp2(s)
            p = e / jnp.sum(e, axis=-1, keepdims=True)
            outs.append(_dot(p.astype(BF16), vm_ref[:, sl]).astype(BF16))
        halves.append(jnp.concatenate(outs, axis=1))
    att = jnp.concatenate(halves, axis=0)
    o_ref[...] = x1 + _dot(att, wxo_ref[...])


def _mix_xattn(x2, hb, gate, o_f, o_b, kt, vm, convw, convg, gmat, glag, wout, xag, wxq, wxo, seq):
    n = x2.shape[0]
    tm = MIX_TILE
    tiles_per_seq = seq // tm
    halo_per_tile = tm // BF16_SUBLANES
    n_halo = n // BF16_SUBLANES
    const = lambda i: (0, 0)
    row = lambda i: (i, 0)
    return pl.pallas_call(
        functools.partial(_mix_xattn_kernel, tiles_per_seq),
        out_shape=jax.ShapeDtypeStruct((n, D_MODEL), F32),
        grid=(n // tm,),
        in_specs=[pl.BlockSpec((tm, D_MODEL), row),
                  pl.BlockSpec((tm, 2 * CONV_WIDTH), row),
                  pl.BlockSpec((BF16_SUBLANES, CONV_WIDTH),
                               lambda i: (jnp.maximum(i * halo_per_tile - 1, 0), 0)),
                  pl.BlockSpec((BF16_SUBLANES, CONV_WIDTH),
                               lambda i: (jnp.minimum((i + 1) * halo_per_tile, n_halo - 1), 0)),
                  pl.BlockSpec((tm, GLA_V_TOTAL), row),
                  pl.BlockSpec((tm, GLA_V_TOTAL), row),
                  pl.BlockSpec((tm, GLA_V_TOTAL), row),
                  pl.BlockSpec((D_MODEL, N_MEM), lambda i: (i // tiles_per_seq, 0)),
                  pl.BlockSpec((N_MEM, D_MODEL), lambda i: (i // tiles_per_seq, 0)),
                  pl.BlockSpec(convw.shape, const),
                  pl.BlockSpec(convg.shape, const),
                  pl.BlockSpec(gmat.shape, const),
                  pl.BlockSpec(glag.shape, const),
                  pl.BlockSpec(wout.shape, const),
                  pl.BlockSpec(xag.shape, const),
                  pl.BlockSpec(wxq.shape, const),
                  pl.BlockSpec(wxo.shape, const)],
        out_specs=pl.BlockSpec((tm, D_MODEL), row),
        compiler_params=_params(("arbitrary",)),
        name="mix_xattn",
    )(x2, hb, hb, hb, gate, o_f, o_b, kt, vm, convw, convg, gmat, glag, wout, xag, wxq, wxo)


def _mlp_kernel(x_ref, gain_ref, wu_ref, wd_ref, fgain_ref, o_ref):
    x = x_ref[...]
    r = _inv_rms(x)
    xg = (x * gain_ref[...]).astype(BF16)
    acc = None
    for c in range(D_FF // FF_CHUNK):
        sl = slice(c * FF_CHUNK, (c + 1) * FF_CHUNK)
        u = jnp.maximum(_dot(xg, wu_ref[:, sl]), 0.0)
        part = _dot((u * u).astype(BF16), wd_ref[sl, :])
        acc = part if acc is None else acc + part
    o_ref[...] = _rms(x + acc * (r * r), fgain_ref[...])


def _mlp(x2, gain, wu, wd, fgain):
    n = x2.shape[0]
    const = lambda i: (0, 0)
    row = lambda i: (i, 0)
    resident = lambda shape: pl.BlockSpec(shape, const, pipeline_mode=pl.Buffered(1))
    return pl.pallas_call(
        _mlp_kernel,
        out_shape=jax.ShapeDtypeStruct((n, D_MODEL), F32),
        grid=(n // MLP_TILE,),
        in_specs=[pl.BlockSpec((MLP_TILE, D_MODEL), row),
                  pl.BlockSpec(gain.shape, const),
                  resident(wu.shape),
                  resident(wd.shape),
                  pl.BlockSpec(fgain.shape, const)],
        out_specs=pl.BlockSpec((MLP_TILE, D_MODEL), row),
        compiler_params=_params(("arbitrary",)),
        name="mlp",
    )(x2, gain, wu, wd, fgain)


def _block_diag_ones(n, block):
    i = jnp.arange(n)
    return (i[:, None] // block) == (i[None, :] // block)


def kernel(x, mem, mix_norm, w_in, conv_w, conv_norm, w_af, b_af, w_ab, b_ab, gla_norm, w_out, xa_norm, mem_norm, w_xq, w_xkv, w_xo, mlp_norm, w_up, w_down, final_norm):
    batch, seq, d = x.shape
    assert d == D_MODEL and seq % GLA_BLOCK == 0 and seq % TOKEN_TILE == 0
    assert mix_norm.shape[0] == 1, "single-layer block"
    n = batch * seq
    x2 = x.reshape(n, d)
    row2 = lambda v: v.reshape(1, -1).astype(F32)

    cw = CONV_WIDTH
    wa = jnp.zeros((LANES, 2 * GLA_K_TOTAL), F32)
    wa = wa.at[0:GLA_LOWRANK, 0:GLA_K_TOTAL].set(w_af[0])
    wa = wa.at[GLA_LOWRANK:2 * GLA_LOWRANK, GLA_K_TOTAL:].set(w_ab[0]).astype(BF16)
    ba = jnp.concatenate([b_af[0], b_ab[0]]).reshape(1, -1).astype(F32)
    ci = jnp.arange(CUMSUM_BLOCK)
    same_chunk = _block_diag_ones(CUMSUM_BLOCK, GLA_CHUNK)
    tril = (same_chunk & (ci[None, :] <= ci[:, None])).astype(BF16)
    triu = (same_chunk & (ci[None, :] >= ci[:, None])).astype(BF16)
    gmat = (_block_diag_ones(cw, CONV_GROUP).astype(F32) / CONV_GROUP).astype(BF16)
    hb, qkv, gate, la = _in_proj(x2, row2(mix_norm[0]), w_in[0].T, wa, ba)
    kt, vm = _mem_kv(mem.reshape(batch * N_MEM, d), row2(mem_norm[0]), w_xkv[0].astype(BF16))
    o_f, o_b = _gla(qkv, la, tril, triu, batch, seq)
    xa = _mix_xattn(x2, hb, gate, o_f, o_b, kt, vm, conv_w[0].astype(F32), row2(conv_norm[0]), gmat,
                    row2(gla_norm[0]), w_out[0].astype(BF16), row2(xa_norm[0]),
                    w_xq[0].astype(BF16), w_xo[0].astype(BF16), seq)
    out = _mlp(xa, row2(mlp_norm[0]), w_up[0].astype(BF16), w_down[0].astype(BF16),
               row2(final_norm))
    return out.reshape(batch, seq, d)
```

```python
import functools

import jax
import jax.numpy as jnp
from jax import lax
from jax.experimental import pallas as pl
from jax.experimental.pallas import tpu as pltpu

F32 = jnp.float32
BF16 = jnp.bfloat16

EPS = 1e-6
LOG2_E = 1.4426950408889634
D_MODEL = 1024
CONV_WIDTH = 512
CONV_GROUP = 64
GLA_HEADS = 4
GLA_DK = 64
GLA_DV = 128
GLA_K_TOTAL = GLA_HEADS * GLA_DK
GLA_V_TOTAL = GLA_HEADS * GLA_DV
GLA_LOWRANK = 16
GLA_GATE_NORM = 16.0
GLA_CHUNK = 64
N_MEM = 256
XA_HEADS = 4
XA_HEAD_DIM = D_MODEL // XA_HEADS
D_FF = 4 * D_MODEL

LANES = 128
BF16_SUBLANES = 16
VMEM_LIMIT_BYTES = 56 * 1024 * 1024

TOKEN_TILE = 1024
MIX_TILE = 1024
MLP_TILE = 1024
XA_ROW_SPLIT = 4
WEIGHT_PREP_ROWS = 128
GLA_BLOCK = 1024
CUMSUM_BLOCK = 256
FF_CHUNK = 4096


def _dot(a, b):
    return jnp.dot(a, b, preferred_element_type=F32)


def _dot_nt(a, b):
    return lax.dot_general(a, b, (((1,), (1,)), ((), ())), preferred_element_type=F32)


def _dot_tn(a, b):
    return lax.dot_general(a, b, (((0,), (0,)), ((), ())), preferred_element_type=F32)


def _inv_rms(x):
    return lax.rsqrt(jnp.mean(x * x, axis=-1, keepdims=True) + EPS)


def _rms(x, gain):
    return x * _inv_rms(x) * gain


def _params(semantics):
    return pltpu.CompilerParams(dimension_semantics=semantics,
                                vmem_limit_bytes=VMEM_LIMIT_BYTES)


def _in_proj_kernel(x_ref, gain_ref, wt_ref, wa_ref, ba_ref, hb_ref, qkv_ref, gate_ref, la_ref,
                    wc_ref, wg_ref):
    cw = CONV_WIDTH
    o_q = 3 * cw
    o_k = o_q + GLA_K_TOTAL
    o_lr = o_q + 2 * GLA_K_TOTAL + 2 * GLA_V_TOTAL

    @pl.when(pl.program_id(0) == 0)
    def _():
        def convert(dst_ref, dst0, src0, rows, scale=None):
            for r in range(0, rows, WEIGHT_PREP_ROWS):
                nr = min(WEIGHT_PREP_ROWS, rows - r)
                blk = wt_ref[src0 + r:src0 + r + nr, :]
                if scale is not None:
                    blk = blk * scale
                dst_ref[dst0 + r:dst0 + r + nr, :] = blk.astype(BF16)

        convert(wc_ref, 0, cw, 2 * cw)
        convert(wc_ref, 2 * cw, 0, cw)
        wg_ref[0:LANES, :] = jnp.zeros((LANES, D_MODEL), BF16)
        convert(wg_ref, 0, o_lr, 2 * GLA_LOWRANK)
        convert(wg_ref, LANES, o_q, GLA_K_TOTAL, GLA_DK ** -0.5)
        convert(wg_ref, LANES + GLA_K_TOTAL, o_k, o_lr - o_k)

    x = x_ref[...]
    r = _inv_rms(x)
    xg = (x * gain_ref[...]).astype(BF16)
    qkv_cols = 2 * GLA_K_TOTAL + GLA_V_TOTAL
    zc = _dot_nt(xg, wc_ref[...])
    hb_ref[:, 0:cw] = (zc[:, 0:cw] * zc[:, cw:2 * cw] * (r * r)).astype(BF16)
    hb_ref[:, cw:2 * cw] = (zc[:, 2 * cw:3 * cw] * r).astype(BF16)
    zg = _dot_nt(xg, wg_ref[...]) * r
    qkv_ref[...] = zg[:, LANES:LANES + qkv_cols].astype(BF16)
    g = zg[:, LANES + qkv_cols:]
    gate_ref[...] = (g / (1.0 + jnp.exp(-g))).astype(BF16)
    codes = zg[:, 0:LANES].astype(BF16)
    z = _dot(codes, wa_ref[...]) + ba_ref[...]
    l2a = ((jnp.minimum(z, 0.0) - jnp.log(1.0 + jnp.exp(-jnp.abs(z))))
           * (LOG2_E / GLA_GATE_NORM))
    la_ref[...] = l2a.astype(BF16)


def _in_proj(x2, gain, w_in_t, wa, ba):
    n = x2.shape[0]
    const = lambda i: (0, 0)
    row = lambda i: (i, 0)
    conv_cols = 3 * CONV_WIDTH
    gla_cols = 2 * GLA_K_TOTAL + 2 * GLA_V_TOTAL
    assert w_in_t.shape == (conv_cols + gla_cols + 2 * GLA_LOWRANK, D_MODEL)
    outs = (2 * CONV_WIDTH, 2 * GLA_K_TOTAL + GLA_V_TOTAL, GLA_V_TOTAL, 2 * GLA_K_TOTAL)
    return pl.pallas_call(
        _in_proj_kernel,
        out_shape=tuple(jax.ShapeDtypeStruct((n, c), BF16) for c in outs),
        grid=(n // TOKEN_TILE,),
        in_specs=[pl.BlockSpec((TOKEN_TILE, D_MODEL), row),
                  pl.BlockSpec(gain.shape, const),
                  pl.BlockSpec(w_in_t.shape, const, pipeline_mode=pl.Buffered(1)),
                  pl.BlockSpec(wa.shape, const),
                  pl.BlockSpec(ba.shape, const)],
        out_specs=tuple(pl.BlockSpec((TOKEN_TILE, c), row) for c in outs),
        scratch_shapes=[pltpu.VMEM((c, D_MODEL), BF16) for c in (conv_cols, LANES + gla_cols)],
        compiler_params=_params(("arbitrary",)),
        name="in_proj",
    )(x2, gain, w_in_t, wa, ba)


def _mem_kv_kernel(m_ref, gain_ref, w_ref, kt_ref, v_ref):
    h = _rms(m_ref[...], gain_ref[...]).astype(BF16)
    kv = _dot(h, w_ref[...])
    kt_ref[...] = kv[:, 0:D_MODEL].T.astype(BF16)
    v_ref[...] = kv[:, D_MODEL:].astype(BF16)


def _mem_kv(mem2, gain, w):
    n = mem2.shape[0]
    batch = n // N_MEM
    const = lambda i: (0, 0)
    row = lambda i: (i, 0)
    return pl.pallas_call(
        _mem_kv_kernel,
        out_shape=(jax.ShapeDtypeStruct((batch * D_MODEL, N_MEM), BF16),
                   jax.ShapeDtypeStruct((n, D_MODEL), BF16)),
        grid=(batch,),
        in_specs=[pl.BlockSpec((N_MEM, D_MODEL), row),
                  pl.BlockSpec(gain.shape, const),
                  pl.BlockSpec(w.shape, const)],
        out_specs=(pl.BlockSpec((D_MODEL, N_MEM), row),
                   pl.BlockSpec((N_MEM, D_MODEL), row)),
        compiler_params=_params(("arbitrary",)),
        name="mem_kv",
    )(mem2, gain, w)


def _gla_prepare(qkv_ref, la_ref, tri_ref, qt_s, kt_s, kh_s, dec_s, reverse):
    tri = tri_ref[...]
    for r in range(GLA_BLOCK // CUMSUM_BLOCK):
        b = _dot(tri, la_ref[r * CUMSUM_BLOCK:(r + 1) * CUMSUM_BLOCK, :])
        grow = jnp.exp2(b)
        shrink = jnp.exp2(-b)
        rows = slice(r * CUMSUM_BLOCK, (r + 1) * CUMSUM_BLOCK)
        qt_s[rows, :] = qkv_ref[rows, 0:GLA_K_TOTAL] * grow.astype(BF16)
        kt = qkv_ref[rows, GLA_K_TOTAL:2 * GLA_K_TOTAL] * shrink.astype(BF16)
        kt_s[rows, :] = kt
        for cc in range(CUMSUM_BLOCK // GLA_CHUNK):
            c = r * (CUMSUM_BLOCK // GLA_CHUNK) + cc
            last = cc * GLA_CHUNK + (0 if reverse else GLA_CHUNK - 1)
            dec = grow[last:last + 1, :]
            dec_s[c] = dec
            crow = slice(cc * GLA_CHUNK, (cc + 1) * GLA_CHUNK)
            kh_s[c * GLA_CHUNK:(c + 1) * GLA_CHUNK, :] = kt[crow] * dec.astype(BF16)


def _gla_chunk(c, qkv_ref, o_ref, qt_s, kt_s, kh_s, dec_s, st_ref, reverse):
    rows = slice(c * GLA_CHUNK, (c + 1) * GLA_CHUNK)
    lane = lax.broadcasted_iota(jnp.int32, (GLA_CHUNK, LANES), 1)
    row = lax.broadcasted_iota(jnp.int32, (GLA_CHUNK, LANES), 0)
    first_head = lane < GLA_DK
    src = lane & (GLA_DK - 1)
    keep = (src > row) if reverse else (src <= row)
    vlane = lax.broadcasted_iota(jnp.int32, (GLA_CHUNK, 2 * GLA_DV), 1)
    first_head_v = vlane < GLA_DV
    dec = dec_s[c]
    zero = jnp.zeros((), BF16)
    for p in range(GLA_HEADS // 2):
        ksl = slice(p * LANES, (p + 1) * LANES)
        osl = slice(p * 2 * GLA_DV, (p + 1) * 2 * GLA_DV)
        vsl = slice(2 * GLA_K_TOTAL + p * 2 * GLA_DV, 2 * GLA_K_TOTAL + (p + 1) * 2 * GLA_DV)
        qc = qt_s[rows, ksl]
        kc = kt_s[rows, ksl]
        khc = kh_s[rows, ksl]
        vc = qkv_ref[rows, vsl]
        kbd = jnp.concatenate([jnp.where(first_head, kc, zero),
                               jnp.where(first_head, zero, kc)], axis=0)
        khbd = jnp.concatenate([jnp.where(first_head, khc, zero),
                                jnp.where(first_head, zero, khc)], axis=0)
        vbd = jnp.concatenate([jnp.where(first_head_v, vc, zero),
                               jnp.where(first_head_v, zero, vc)], axis=0)
        scores = _dot_nt(qc, kbd)
        scores = jnp.where(keep, scores, 0.0).astype(BF16)
        st = st_ref[p]
        o = _dot(scores, vbd) + _dot_nt(qc, st.astype(BF16))
        st_ref[p] = st * dec[:, ksl] + _dot_tn(vbd, khbd)
        o_ref[rows, osl] = o.astype(o_ref.dtype)


def _gla_kernel(qkv_f_ref, la_f_ref, qkv_b_ref, la_b_ref, tril_ref, triu_ref, of_ref, ob_ref,
                qt_f, kt_f, kh_f, dec_f, st_f, qt_b, kt_b, kh_b, dec_b, st_b):
    nchunk = GLA_BLOCK // GLA_CHUNK

    @pl.when(pl.program_id(1) == 0)
    def _():
        st_f[...] = jnp.zeros_like(st_f)
        st_b[...] = jnp.zeros_like(st_b)

    _gla_prepare(qkv_f_ref, la_f_ref, tril_ref, qt_f, kt_f, kh_f, dec_f, False)
    _gla_prepare(qkv_b_ref, la_b_ref, triu_ref, qt_b, kt_b, kh_b, dec_b, True)

    for c in range(nchunk):
        _gla_chunk(c, qkv_f_ref, of_ref, qt_f, kt_f, kh_f, dec_f, st_f, False)
        _gla_chunk(nchunk - 1 - c, qkv_b_ref, ob_ref, qt_b, kt_b, kh_b, dec_b, st_b, True)


def _gla(qkv, la, tril, triu, batch, seq):
    n = qkv.shape[0]
    nb = seq // GLA_BLOCK
    nchunk = GLA_BLOCK // GLA_CHUNK
    qkv_cols = 2 * GLA_K_TOTAL + GLA_V_TOTAL
    fwd = lambda b, i: (b * nb + i, 0)
    bwd = lambda b, i: (b * nb + nb - 1 - i, 0)
    bwd_la = lambda b, i: (b * nb + nb - 1 - i, 1)
    const = lambda b, i: (0, 0)
    dir_scratch = [pltpu.VMEM((GLA_BLOCK, GLA_K_TOTAL), BF16),
                   pltpu.VMEM((GLA_BLOCK, GLA_K_TOTAL), BF16),
                   pltpu.VMEM((GLA_BLOCK, GLA_K_TOTAL), BF16),
                   pltpu.VMEM((nchunk, 1, GLA_K_TOTAL), F32),
                   pltpu.VMEM((GLA_HEADS // 2, 2 * GLA_DV, LANES), F32)]
    return pl.pallas_call(
        _gla_kernel,
        out_shape=(jax.ShapeDtypeStruct((n, GLA_V_TOTAL), BF16),
                   jax.ShapeDtypeStruct((n, GLA_V_TOTAL), BF16)),
        grid=(batch, nb),
        in_specs=[pl.BlockSpec((GLA_BLOCK, qkv_cols), fwd),
                  pl.BlockSpec((GLA_BLOCK, GLA_K_TOTAL), fwd),
                  pl.BlockSpec((GLA_BLOCK, qkv_cols), bwd),
                  pl.BlockSpec((GLA_BLOCK, GLA_K_TOTAL), bwd_la),
                  pl.BlockSpec(tril.shape, const),
                  pl.BlockSpec(triu.shape, const)],
        out_specs=(pl.BlockSpec((GLA_BLOCK, GLA_V_TOTAL), fwd),
                   pl.BlockSpec((GLA_BLOCK, GLA_V_TOTAL), bwd)),
        scratch_shapes=dir_scratch + dir_scratch,
        compiler_params=_params(("arbitrary", "arbitrary")),
        name="gla",
    )(qkv, la, qkv, la, tril, triu)


def _mix_xattn_kernel(tiles_per_seq, x_ref, hb_ref, prev_ref, next_ref, gate_ref, of_ref, ob_ref,
                      kt_ref, vm_ref, convw_ref, convg_ref, gmat_ref, glag_ref, wout_ref,
                      xag_ref, wxq_ref, wxo_ref, o_ref):
    tm = x_ref.shape[0]
    t = pl.program_id(0) % tiles_per_seq
    has_prev = (t > 0).astype(F32)
    has_next = (t < tiles_per_seq - 1).astype(F32)

    h = hb_ref[:, 0:CONV_WIDTH].astype(F32)
    last = BF16_SUBLANES - 1
    h_before = prev_ref[last:last + 1, :].astype(F32) * has_prev
    h_after = next_ref[0:1, :].astype(F32) * has_next
    rowid = lax.broadcasted_iota(jnp.int32, (tm, CONV_WIDTH), 0)
    h_m1 = jnp.where(rowid == 0, h_before, pltpu.roll(h, 1, 0))
    h_p1 = jnp.where(rowid == tm - 1, h_after, pltpu.roll(h, tm - 1, 0))
    y = convw_ref[0:1, :] * h_m1 + convw_ref[1:2, :] * h + convw_ref[2:3, :] * h_p1
    y = hb_ref[:, CONV_WIDTH:2 * CONV_WIDTH].astype(F32) * y
    ms = _dot((y * y).astype(BF16), gmat_ref[...])
    y = y * lax.rsqrt(ms + EPS) * convg_ref[...]

    o = of_ref[...].astype(F32) + ob_ref[...].astype(F32)
    heads = []
    for hd in range(GLA_HEADS):
        oh = o[:, hd * GLA_DV:(hd + 1) * GLA_DV]
        heads.append(_rms(oh, glag_ref[...]))
    o = jnp.concatenate(heads, axis=1) * gate_ref[...].astype(F32)

    mixed = jnp.concatenate([y.astype(BF16), o.astype(BF16)], axis=1)
    x1 = x_ref[...] + _dot(mixed, wout_ref[...])

    r = _inv_rms(x1) * (XA_HEAD_DIM ** -0.5 * LOG2_E)
    q = (_dot((x1 * xag_ref[...]).astype(BF16), wxq_ref[...]) * r).astype(BF16)
    half_rows = tm // XA_ROW_SPLIT
    halves = []
    for top in range(0, tm, half_rows):
        rows = slice(top, top + half_rows)
        outs = []
        for hd in range(XA_HEADS):
            sl = slice(hd * XA_HEAD_DIM, (hd + 1) * XA_HEAD_DIM)
            s = _dot(q[rows, sl], kt_ref[sl, :])
            s = s - jnp.max(s, axis=-1, keepdims=True)
            e = jnp.exp2(s)
            p = e / jnp.sum(e, axis=-1, keepdims=True)
            outs.append(_dot(p.astype(BF16), vm_ref[:, sl]).astype(BF16))
        halves.append(jnp.concatenate(outs, axis=1))
    att = jnp.concatenate(halves, axis=0)
    o_ref[...] = x1 + _dot(att, wxo_ref[...])


def _mix_xattn(x2, hb, gate, o_f, o_b, kt, vm, convw, convg, gmat, glag, wout, xag, wxq, wxo, seq):
    n = x2.shape[0]
    tm = MIX_TILE
    tiles_per_seq = seq // tm
    halo_per_tile = tm // BF16_SUBLANES
    n_halo = n // BF16_SUBLANES
    const = lambda i: (0, 0)
    row = lambda i: (i, 0)
    return pl.pallas_call(
        functools.partial(_mix_xattn_kernel, tiles_per_seq),
        out_shape=jax.ShapeDtypeStruct((n, D_MODEL), F32),
        grid=(n // tm,),
        in_specs=[pl.BlockSpec((tm, D_MODEL), row),
                  pl.BlockSpec((tm, 2 * CONV_WIDTH), row),
                  pl.BlockSpec((BF16_SUBLANES, CONV_WIDTH),
                               lambda i: (jnp.maximum(i * halo_per_tile - 1, 0), 0)),
                  pl.BlockSpec((BF16_SUBLANES, CONV_WIDTH),
                               lambda i: (jnp.minimum((i + 1) * halo_per_tile, n_halo - 1), 0)),
                  pl.BlockSpec((tm, GLA_V_TOTAL), row),
                  pl.BlockSpec((tm, GLA_V_TOTAL), row),
                  pl.BlockSpec((tm, GLA_V_TOTAL), row),
                  pl.BlockSpec((D_MODEL, N_MEM), lambda i: (i // tiles_per_seq, 0)),
                  pl.BlockSpec((N_MEM, D_MODEL), lambda i: (i // tiles_per_seq, 0)),
                  pl.BlockSpec(convw.shape, const),
                  pl.BlockSpec(convg.shape, const),
                  pl.BlockSpec(gmat.shape, const),
                  pl.BlockSpec(glag.shape, const),
                  pl.BlockSpec(wout.shape, const),
                  pl.BlockSpec(xag.shape, const),
                  pl.BlockSpec(wxq.shape, const),
                  pl.BlockSpec(wxo.shape, const)],
        out_specs=pl.BlockSpec((tm, D_MODEL), row),
        compiler_params=_params(("arbitrary",)),
        name="mix_xattn",
    )(x2, hb, hb, hb, gate, o_f, o_b, kt, vm, convw, convg, gmat, glag, wout, xag, wxq, wxo)


def _mlp_kernel(x_ref, gain_ref, wu_ref, wd_ref, fgain_ref, o_ref):
    x = x_ref[...]
    r = _inv_rms(x)
    xg = (x * gain_ref[...]).astype(BF16)
    acc = None
    for c in range(D_FF // FF_CHUNK):
        sl = slice(c * FF_CHUNK, (c + 1) * FF_CHUNK)
        u = jnp.maximum(_dot(xg, wu_ref[:, sl]), 0.0)
        part = _dot((u * u).astype(BF16), wd_ref[sl, :])
        acc = part if acc is None else acc + part
    o_ref[...] = _rms(x + acc * (r * r), fgain_ref[...])


def _mlp(x2, gain, wu, wd, fgain):
    n = x2.shape[0]
    const = lambda i: (0, 0)
    row = lambda i: (i, 0)
    resident = lambda shape: pl.BlockSpec(shape, const, pipeline_mode=pl.Buffered(1))
    return pl.pallas_call(
        _mlp_kernel,
        out_shape=jax.ShapeDtypeStruct((n, D_MODEL), F32),
        grid=(n // MLP_TILE,),
        in_specs=[pl.BlockSpec((MLP_TILE, D_MODEL), row),
                  pl.BlockSpec(gain.shape, const),
                  resident(wu.shape),
                  resident(wd.shape),
                  pl.BlockSpec(fgain.shape, const)],
        out_specs=pl.BlockSpec((MLP_TILE, D_MODEL), row),
        compiler_params=_params(("arbitrary",)),
        name="mlp",
    )(x2, gain, wu, wd, fgain)


def _block_diag_ones(n, block):
    i = jnp.arange(n)
    return (i[:, None] // block) == (i[None, :] // block)


def kernel(x, mem, mix_norm, w_in, conv_w, conv_norm, w_af, b_af, w_ab, b_ab, gla_norm, w_out, xa_norm, mem_norm, w_xq, w_xkv, w_xo, mlp_norm, w_up, w_down, final_norm):
    batch, seq, d = x.shape
    assert d == D_MODEL and seq % GLA_BLOCK == 0 and seq % TOKEN_TILE == 0
    assert seq % MIX_TILE == 0 and (batch * seq) % MLP_TILE == 0
    assert mix_norm.shape[0] == 1, "single-layer block"
    n = batch * seq
    x2 = x.reshape(n, d)
    row2 = lambda v: v.reshape(1, -1).astype(F32)

    cw = CONV_WIDTH
    wa = jnp.zeros((LANES, 2 * GLA_K_TOTAL), F32)
    wa = wa.at[0:GLA_LOWRANK, 0:GLA_K_TOTAL].set(w_af[0])
    wa = wa.at[GLA_LOWRANK:2 * GLA_LOWRANK, GLA_K_TOTAL:].set(w_ab[0]).astype(BF16)
    ba = jnp.concatenate([b_af[0], b_ab[0]]).reshape(1, -1).astype(F32)
    ci = jnp.arange(CUMSUM_BLOCK)
    same_chunk = _block_diag_ones(CUMSUM_BLOCK, GLA_CHUNK)
    tril = (same_chunk & (ci[None, :] <= ci[:, None])).astype(BF16)
    triu = (same_chunk & (ci[None, :] >= ci[:, None])).astype(BF16)
    gmat = (_block_diag_ones(cw, CONV_GROUP).astype(F32) / CONV_GROUP).astype(BF16)

    hb, qkv, gate, la = _in_proj(x2, row2(mix_norm[0]), w_in[0].T, wa, ba)
    kt, vm = _mem_kv(mem.reshape(batch * N_MEM, d), row2(mem_norm[0]), w_xkv[0].astype(BF16))
    o_f, o_b = _gla(qkv, la, tril, triu, batch, seq)
    xa = _mix_xattn(x2, hb, gate, o_f, o_b, kt, vm, conv_w[0].astype(F32), row2(conv_norm[0]), gmat,
                    row2(gla_norm[0]), w_out[0].astype(BF16), row2(xa_norm[0]),
                    w_xq[0].astype(BF16), w_xo[0].astype(BF16), seq)
    out = _mlp(xa, row2(mlp_norm[0]), w_up[0].astype(BF16), w_down[0].astype(BF16),
               row2(final_norm))
    return out.reshape(batch, seq, d)
```

```python
import functools

import jax
import jax.numpy as jnp
import numpy as np
from jax import lax
from jax.experimental import pallas as pl
from jax.experimental.pallas import tpu as pltpu

F32 = jnp.float32
BF16 = jnp.bfloat16

EPS = 1e-6
LOG2_E = 1.4426950408889634
D_MODEL = 1024
CONV_WIDTH = 512
CONV_GROUP = 64
GLA_HEADS = 4
GLA_DK = 64
GLA_DV = 128
GLA_K_TOTAL = GLA_HEADS * GLA_DK
GLA_V_TOTAL = GLA_HEADS * GLA_DV
GLA_LOWRANK = 16
GLA_GATE_NORM = 16.0
GLA_CHUNK = 64
N_MEM = 256
XA_HEADS = 4
XA_HEAD_DIM = D_MODEL // XA_HEADS
D_FF = 4 * D_MODEL

LANES = 128
BF16_SUBLANES = 16
VMEM_LIMIT_BYTES = 56 * 1024 * 1024

TOKEN_TILE = 1024
MIX_TILE = 1024
MLP_TILE = 1024
XA_ROW_SPLIT = 4
WEIGHT_PREP_ROWS = 128
GLA_BLOCK = 1024
CUMSUM_BLOCK = 256
FF_CHUNK = 4096


def _dot(a, b):
    return jnp.dot(a, b, preferred_element_type=F32)


def _dot_nt(a, b):
    return lax.dot_general(a, b, (((1,), (1,)), ((), ())), preferred_element_type=F32)


def _dot_tn(a, b):
    return lax.dot_general(a, b, (((0,), (0,)), ((), ())), preferred_element_type=F32)


def _inv_rms(x):
    return lax.rsqrt(jnp.mean(x * x, axis=-1, keepdims=True) + EPS)


def _rms(x, gain):
    return x * _inv_rms(x) * gain


def _params(semantics):
    return pltpu.CompilerParams(dimension_semantics=semantics,
                                vmem_limit_bytes=VMEM_LIMIT_BYTES)


def _in_proj_kernel(x_ref, gain_ref, wt_ref, waf_ref, wab_ref, baf_ref, bab_ref,
                    hb_ref, qkv_ref, gate_ref, la_ref, wc_ref, wg_ref, wa_ref):
    cw = CONV_WIDTH
    o_q = 3 * cw
    o_k = o_q + GLA_K_TOTAL
    o_lr = o_q + 2 * GLA_K_TOTAL + 2 * GLA_V_TOTAL

    @pl.when(pl.program_id(0) == 0)
    def _():
        def convert(dst_ref, dst0, src0, rows, scale=None):
            for r in range(0, rows, WEIGHT_PREP_ROWS):
                nr = min(WEIGHT_PREP_ROWS, rows - r)
                blk = wt_ref[src0 + r:src0 + r + nr, :]
                if scale is not None:
                    blk = blk * scale
                dst_ref[dst0 + r:dst0 + r + nr, :] = blk.astype(BF16)

        convert(wc_ref, 0, cw, 2 * cw)
        convert(wc_ref, 2 * cw, 0, cw)
        wg_ref[0:LANES, :] = jnp.zeros((LANES, D_MODEL), BF16)
        convert(wg_ref, 0, o_lr, 2 * GLA_LOWRANK)
        convert(wg_ref, LANES, o_q, GLA_K_TOTAL, GLA_DK ** -0.5)
        convert(wg_ref, LANES + GLA_K_TOTAL, o_k, o_lr - o_k)
        wa_ref[...] = jnp.zeros(wa_ref.shape, BF16)
        wa_ref[0:GLA_LOWRANK, 0:GLA_K_TOTAL] = waf_ref[...].astype(BF16)
        wa_ref[GLA_LOWRANK:2 * GLA_LOWRANK, GLA_K_TOTAL:] = wab_ref[...].astype(BF16)

    x = x_ref[...]
    r = _inv_rms(x)
    xg = (x * gain_ref[...]).astype(BF16)
    qkv_cols = 2 * GLA_K_TOTAL + GLA_V_TOTAL
    zc = _dot_nt(xg, wc_ref[...])
    hb_ref[:, 0:cw] = (zc[:, 0:cw] * zc[:, cw:2 * cw] * (r * r)).astype(BF16)
    hb_ref[:, cw:2 * cw] = (zc[:, 2 * cw:3 * cw] * r).astype(BF16)
    zg = _dot_nt(xg, wg_ref[...]) * r
    qkv_ref[...] = zg[:, LANES:LANES + qkv_cols].astype(BF16)
    g = zg[:, LANES + qkv_cols:]
    gate_ref[...] = (g / (1.0 + jnp.exp(-g))).astype(BF16)
    codes = zg[:, 0:LANES].astype(BF16)
    z = _dot(codes, wa_ref[...]) + jnp.concatenate([baf_ref[...], bab_ref[...]], axis=1)
    l2a = ((jnp.minimum(z, 0.0) - jnp.log(1.0 + jnp.exp(-jnp.abs(z))))
           * (LOG2_E / GLA_GATE_NORM))
    la_ref[...] = l2a.astype(BF16)


def _in_proj(x2, gain, w_in_t, w_af, w_ab, b_af, b_ab):
    n = x2.shape[0]
    assert w_af.shape == w_ab.shape == (GLA_LOWRANK, GLA_K_TOTAL)
    assert b_af.shape == b_ab.shape == (1, GLA_K_TOTAL)
    const = lambda i: (0, 0)
    row = lambda i: (i, 0)
    conv_cols = 3 * CONV_WIDTH
    gla_cols = 2 * GLA_K_TOTAL + 2 * GLA_V_TOTAL
    assert w_in_t.shape == (conv_cols + gla_cols + 2 * GLA_LOWRANK, D_MODEL)
    outs = (2 * CONV_WIDTH, 2 * GLA_K_TOTAL + GLA_V_TOTAL, GLA_V_TOTAL, 2 * GLA_K_TOTAL)
    return pl.pallas_call(
        _in_proj_kernel,
        out_shape=tuple(jax.ShapeDtypeStruct((n, c), BF16) for c in outs),
        grid=(n // TOKEN_TILE,),
        in_specs=[pl.BlockSpec((TOKEN_TILE, D_MODEL), row),
                  pl.BlockSpec(gain.shape, const),
                  pl.BlockSpec(w_in_t.shape, const, pipeline_mode=pl.Buffered(1)),
                  pl.BlockSpec(w_af.shape, const),
                  pl.BlockSpec(w_ab.shape, const),
                  pl.BlockSpec(b_af.shape, const),
                  pl.BlockSpec(b_ab.shape, const)],
        out_specs=tuple(pl.BlockSpec((TOKEN_TILE, c), row) for c in outs),
        scratch_shapes=[pltpu.VMEM((conv_cols, D_MODEL), BF16),
                        pltpu.VMEM((LANES + gla_cols, D_MODEL), BF16),
                        pltpu.VMEM((LANES, 2 * GLA_K_TOTAL), BF16)],
        compiler_params=_params(("arbitrary",)),
        name="in_proj",
    )(x2, gain, w_in_t, w_af, w_ab, b_af, b_ab)


def _mem_kv_kernel(m_ref, gain_ref, w_ref, kt_ref, v_ref):
    h = _rms(m_ref[...], gain_ref[...]).astype(BF16)
    kv = _dot(h, w_ref[...])
    kt_ref[...] = kv[:, 0:D_MODEL].T.astype(BF16)
    v_ref[...] = kv[:, D_MODEL:].astype(BF16)


def _mem_kv(mem2, gain, w):
    n = mem2.shape[0]
    batch = n // N_MEM
    const = lambda i: (0, 0)
    row = lambda i: (i, 0)
    return pl.pallas_call(
        _mem_kv_kernel,
        out_shape=(jax.ShapeDtypeStruct((batch * D_MODEL, N_MEM), BF16),
                   jax.ShapeDtypeStruct((n, D_MODEL), BF16)),
        grid=(batch,),
        in_specs=[pl.BlockSpec((N_MEM, D_MODEL), row),
                  pl.BlockSpec(gain.shape, const),
                  pl.BlockSpec(w.shape, const)],
        out_specs=(pl.BlockSpec((D_MODEL, N_MEM), row),
                   pl.BlockSpec((N_MEM, D_MODEL), row)),
        compiler_params=_params(("arbitrary",)),
        name="mem_kv",
    )(mem2, gain, w)


def _gla_prepare(qkv_ref, la_ref, tri_ref, qt_s, kt_s, kh_s, dec_s, reverse):
    tri = tri_ref[...]
    for r in range(GLA_BLOCK // CUMSUM_BLOCK):
        b = _dot(tri, la_ref[r * CUMSUM_BLOCK:(r + 1) * CUMSUM_BLOCK, :])
        grow = jnp.exp2(b)
        shrink = jnp.exp2(-b)
        rows = slice(r * CUMSUM_BLOCK, (r + 1) * CUMSUM_BLOCK)
        qt_s[rows, :] = qkv_ref[rows, 0:GLA_K_TOTAL] * grow.astype(BF16)
        kt = qkv_ref[rows, GLA_K_TOTAL:2 * GLA_K_TOTAL] * shrink.astype(BF16)
        kt_s[rows, :] = kt
        for cc in range(CUMSUM_BLOCK // GLA_CHUNK):
            c = r * (CUMSUM_BLOCK // GLA_CHUNK) + cc
            last = cc * GLA_CHUNK + (0 if reverse else GLA_CHUNK - 1)
            dec = grow[last:last + 1, :]
            dec_s[c] = dec
            crow = slice(cc * GLA_CHUNK, (cc + 1) * GLA_CHUNK)
            kh_s[c * GLA_CHUNK:(c + 1) * GLA_CHUNK, :] = kt[crow] * dec.astype(BF16)


def _gla_chunk(c, qkv_ref, o_ref, qt_s, kt_s, kh_s, dec_s, st_ref, reverse):
    rows = slice(c * GLA_CHUNK, (c + 1) * GLA_CHUNK)
    lane = lax.broadcasted_iota(jnp.int32, (GLA_CHUNK, LANES), 1)
    row = lax.broadcasted_iota(jnp.int32, (GLA_CHUNK, LANES), 0)
    first_head = lane < GLA_DK
    src = lane & (GLA_DK - 1)
    keep = (src > row) if reverse else (src <= row)
    vlane = lax.broadcasted_iota(jnp.int32, (GLA_CHUNK, 2 * GLA_DV), 1)
    first_head_v = vlane < GLA_DV
    dec = dec_s[c]
    zero = jnp.zeros((), BF16)
    for p in range(GLA_HEADS // 2):
        ksl = slice(p * LANES, (p + 1) * LANES)
        osl = slice(p * 2 * GLA_DV, (p + 1) * 2 * GLA_DV)
        vsl = slice(2 * GLA_K_TOTAL + p * 2 * GLA_DV, 2 * GLA_K_TOTAL + (p + 1) * 2 * GLA_DV)
        qc = qt_s[rows, ksl]
        kc = kt_s[rows, ksl]
        khc = kh_s[rows, ksl]
        vc = qkv_ref[rows, vsl]
        kbd = jnp.concatenate([jnp.where(first_head, kc, zero),
                               jnp.where(first_head, zero, kc)], axis=0)
        khbd = jnp.concatenate([jnp.where(first_head, khc, zero),
                                jnp.where(first_head, zero, khc)], axis=0)
        vbd = jnp.concatenate([jnp.where(first_head_v, vc, zero),
                               jnp.where(first_head_v, zero, vc)], axis=0)
        scores = _dot_nt(qc, kbd)
        scores = jnp.where(keep, scores, 0.0).astype(BF16)
        st = st_ref[p]
        o = _dot(scores, vbd) + _dot_nt(qc, st.astype(BF16))
        st_ref[p] = st * dec[:, ksl] + _dot_tn(vbd, khbd)
        o_ref[rows, osl] = o.astype(o_ref.dtype)


def _gla_kernel(qkv_f_ref, la_f_ref, qkv_b_ref, la_b_ref, tril_ref, triu_ref, of_ref, ob_ref,
                qt_f, kt_f, kh_f, dec_f, st_f, qt_b, kt_b, kh_b, dec_b, st_b):
    nchunk = GLA_BLOCK // GLA_CHUNK

    @pl.when(pl.program_id(1) == 0)
    def _():
        st_f[...] = jnp.zeros_like(st_f)
        st_b[...] = jnp.zeros_like(st_b)

    _gla_prepare(qkv_f_ref, la_f_ref, tril_ref, qt_f, kt_f, kh_f, dec_f, False)
    _gla_prepare(qkv_b_ref, la_b_ref, triu_ref, qt_b, kt_b, kh_b, dec_b, True)

    for c in range(nchunk):
        _gla_chunk(c, qkv_f_ref, of_ref, qt_f, kt_f, kh_f, dec_f, st_f, False)
        _gla_chunk(nchunk - 1 - c, qkv_b_ref, ob_ref, qt_b, kt_b, kh_b, dec_b, st_b, True)


def _gla(qkv, la, tril, triu, batch, seq):
    n = qkv.shape[0]
    nb = seq // GLA_BLOCK
    nchunk = GLA_BLOCK // GLA_CHUNK
    qkv_cols = 2 * GLA_K_TOTAL + GLA_V_TOTAL
    fwd = lambda b, i: (b * nb + i, 0)
    bwd = lambda b, i: (b * nb + nb - 1 - i, 0)
    bwd_la = lambda b, i: (b * nb + nb - 1 - i, 1)
    const = lambda b, i: (0, 0)
    dir_scratch = [pltpu.VMEM((GLA_BLOCK, GLA_K_TOTAL), BF16),
                   pltpu.VMEM((GLA_BLOCK, GLA_K_TOTAL), BF16),
                   pltpu.VMEM((GLA_BLOCK, GLA_K_TOTAL), BF16),
                   pltpu.VMEM((nchunk, 1, GLA_K_TOTAL), F32),
                   pltpu.VMEM((GLA_HEADS // 2, 2 * GLA_DV, LANES), F32)]
    return pl.pallas_call(
        _gla_kernel,
        out_shape=(jax.ShapeDtypeStruct((n, GLA_V_TOTAL), BF16),
                   jax.ShapeDtypeStruct((n, GLA_V_TOTAL), BF16)),
        grid=(batch, nb),
        in_specs=[pl.BlockSpec((GLA_BLOCK, qkv_cols), fwd),
                  pl.BlockSpec((GLA_BLOCK, GLA_K_TOTAL), fwd),
                  pl.BlockSpec((GLA_BLOCK, qkv_cols), bwd),
                  pl.BlockSpec((GLA_BLOCK, GLA_K_TOTAL), bwd_la),
                  pl.BlockSpec(tril.shape, const),
                  pl.BlockSpec(triu.shape, const)],
        out_specs=(pl.BlockSpec((GLA_BLOCK, GLA_V_TOTAL), fwd),
                   pl.BlockSpec((GLA_BLOCK, GLA_V_TOTAL), bwd)),
        scratch_shapes=dir_scratch + dir_scratch,
        compiler_params=_params(("arbitrary", "arbitrary")),
        name="gla",
    )(qkv, la, qkv, la, tril, triu)


def _mix_xattn_kernel(tiles_per_seq, x_ref, hb_ref, prev_ref, next_ref, gate_ref, of_ref, ob_ref,
                      kt_ref, vm_ref, convw_ref, convg_ref, gmat_ref, glag_ref, wout_ref,
                      xag_ref, wxq_ref, wxo_ref, o_ref):
    tm = x_ref.shape[0]
    t = pl.program_id(0) % tiles_per_seq
    has_prev = (t > 0).astype(F32)
    has_next = (t < tiles_per_seq - 1).astype(F32)

    h = hb_ref[:, 0:CONV_WIDTH].astype(F32)
    last = BF16_SUBLANES - 1
    h_before = prev_ref[last:last + 1, :].astype(F32) * has_prev
    h_after = next_ref[0:1, :].astype(F32) * has_next
    rowid = lax.broadcasted_iota(jnp.int32, (tm, CONV_WIDTH), 0)
    h_m1 = jnp.where(rowid == 0, h_before, pltpu.roll(h, 1, 0))
    h_p1 = jnp.where(rowid == tm - 1, h_after, pltpu.roll(h, tm - 1, 0))
    y = convw_ref[0:1, :] * h_m1 + convw_ref[1:2, :] * h + convw_ref[2:3, :] * h_p1
    y = hb_ref[:, CONV_WIDTH:2 * CONV_WIDTH].astype(F32) * y
    ms = _dot((y * y).astype(BF16), gmat_ref[...])
    y = y * lax.rsqrt(ms + EPS) * convg_ref[...]

    o = of_ref[...].astype(F32) + ob_ref[...].astype(F32)
    heads = []
    for hd in range(GLA_HEADS):
        oh = o[:, hd * GLA_DV:(hd + 1) * GLA_DV]
        heads.append(_rms(oh, glag_ref[...]))
    o = jnp.concatenate(heads, axis=1) * gate_ref[...].astype(F32)

    mixed = jnp.concatenate([y.astype(BF16), o.astype(BF16)], axis=1)
    x1 = x_ref[...] + _dot(mixed, wout_ref[...])

    r = _inv_rms(x1) * (XA_HEAD_DIM ** -0.5 * LOG2_E)
    q = (_dot((x1 * xag_ref[...]).astype(BF16), wxq_ref[...]) * r).astype(BF16)
    half_rows = tm // XA_ROW_SPLIT
    halves = []
    for top in range(0, tm, half_rows):
        rows = slice(top, top + half_rows)
        outs = []
        for hd in range(XA_HEADS):
            sl = slice(hd * XA_HEAD_DIM, (hd + 1) * XA_HEAD_DIM)
            s = _dot(q[rows, sl], kt_ref[sl, :])
            s = s - jnp.max(s, axis=-1, keepdims=True)
            e = jnp.exp2(s)
            p = e / jnp.sum(e, axis=-1, keepdims=True)
            outs.append(_dot(p.astype(BF16), vm_ref[:, sl]).astype(BF16))
        halves.append(jnp.concatenate(outs, axis=1))
    att = jnp.concatenate(halves, axis=0)
    o_ref[...] = x1 + _dot(att, wxo_ref[...])


def _mix_xattn(x2, hb, gate, o_f, o_b, kt, vm, convw, convg, gmat, glag, wout, xag, wxq, wxo, seq):
    n = x2.shape[0]
    tm = MIX_TILE
    tiles_per_seq = seq // tm
    halo_per_tile = tm // BF16_SUBLANES
    n_halo = n // BF16_SUBLANES
    const = lambda i: (0, 0)
    row = lambda i: (i, 0)
    return pl.pallas_call(
        functools.partial(_mix_xattn_kernel, tiles_per_seq),
        out_shape=jax.ShapeDtypeStruct((n, D_MODEL), F32),
        grid=(n // tm,),
        in_specs=[pl.BlockSpec((tm, D_MODEL), row),
                  pl.BlockSpec((tm, 2 * CONV_WIDTH), row),
                  pl.BlockSpec((BF16_SUBLANES, CONV_WIDTH),
                               lambda i: (jnp.maximum(i * halo_per_tile - 1, 0), 0)),
                  pl.BlockSpec((BF16_SUBLANES, CONV_WIDTH),
                               lambda i: (jnp.minimum((i + 1) * halo_per_tile, n_halo - 1), 0)),
                  pl.BlockSpec((tm, GLA_V_TOTAL), row),
                  pl.BlockSpec((tm, GLA_V_TOTAL), row),
                  pl.BlockSpec((tm, GLA_V_TOTAL), row),
                  pl.BlockSpec((D_MODEL, N_MEM), lambda i: (i // tiles_per_seq, 0)),
                  pl.BlockSpec((N_MEM, D_MODEL), lambda i: (i // tiles_per_seq, 0)),
                  pl.BlockSpec(convw.shape, const),
                  pl.BlockSpec(convg.shape, const),
                  pl.BlockSpec(gmat.shape, const),
                  pl.BlockSpec(glag.shape, const),
                  pl.BlockSpec(wout.shape, const),
                  pl.BlockSpec(xag.shape, const),
                  pl.BlockSpec(wxq.shape, const),
                  pl.BlockSpec(wxo.shape, const)],
        out_specs=pl.BlockSpec((tm, D_MODEL), row),
        compiler_params=_params(("arbitrary",)),
        name="mix_xattn",
    )(x2, hb, hb, hb, gate, o_f, o_b, kt, vm, convw, convg, gmat, glag, wout, xag, wxq, wxo)


def _mlp_kernel(x_ref, gain_ref, wu_ref, wd_ref, fgain_ref, o_ref):
    x = x_ref[...]
    r = _inv_rms(x)
    xg = (x * gain_ref[...]).astype(BF16)
    acc = None
    for c in range(D_FF // FF_CHUNK):
        sl = slice(c * FF_CHUNK, (c + 1) * FF_CHUNK)
        u = jnp.maximum(_dot(xg, wu_ref[:, sl]), 0.0)
        part = _dot((u * u).astype(BF16), wd_ref[sl, :])
        acc = part if acc is None else acc + part
    o_ref[...] = _rms(x + acc * (r * r), fgain_ref[...])


def _mlp(x2, gain, wu, wd, fgain):
    n = x2.shape[0]
    const = lambda i: (0, 0)
    row = lambda i: (i, 0)
    resident = lambda shape: pl.BlockSpec(shape, const, pipeline_mode=pl.Buffered(1))
    return pl.pallas_call(
        _mlp_kernel,
        out_shape=jax.ShapeDtypeStruct((n, D_MODEL), F32),
        grid=(n // MLP_TILE,),
        in_specs=[pl.BlockSpec((MLP_TILE, D_MODEL), row),
                  pl.BlockSpec(gain.shape, const),
                  resident(wu.shape),
                  resident(wd.shape),
                  pl.BlockSpec(fgain.shape, const)],
        out_specs=pl.BlockSpec((MLP_TILE, D_MODEL), row),
        compiler_params=_params(("arbitrary",)),
        name="mlp",
    )(x2, gain, wu, wd, fgain)


def _block_diag_ones(n, block):
    i = np.arange(n)
    return (i[:, None] // block) == (i[None, :] // block)


def _constant(values):
    return jnp.asarray(np.asarray(values, np.float32).astype(BF16))


def kernel(x, mem, mix_norm, w_in, conv_w, conv_norm, w_af, b_af, w_ab, b_ab, gla_norm, w_out, xa_norm, mem_norm, w_xq, w_xkv, w_xo, mlp_norm, w_up, w_down, final_norm):
    batch, seq, d = x.shape
    assert d == D_MODEL and seq % GLA_BLOCK == 0 and seq % TOKEN_TILE == 0
    assert seq % MIX_TILE == 0 and (batch * seq) % MLP_TILE == 0
    assert mix_norm.shape[0] == 1, "single-layer block"
    n = batch * seq
    x2 = x.reshape(n, d)
    row2 = lambda v: v.reshape(1, -1).astype(F32)

    cw = CONV_WIDTH
    ci = np.arange(CUMSUM_BLOCK)
    same_chunk = _block_diag_ones(CUMSUM_BLOCK, GLA_CHUNK)
    tril = _constant(same_chunk & (ci[None, :] <= ci[:, None]))
    triu = _constant(same_chunk & (ci[None, :] >= ci[:, None]))
    gmat = _constant(_block_diag_ones(cw, CONV_GROUP) / CONV_GROUP)

    hb, qkv, gate, la = _in_proj(x2, row2(mix_norm[0]), w_in[0].T, w_af[0], w_ab[0],
                                 row2(b_af[0]), row2(b_ab[0]))
    kt, vm = _mem_kv(mem.reshape(batch * N_MEM, d), row2(mem_norm[0]), w_xkv[0].astype(BF16))
    o_f, o_b = _gla(qkv, la, tril, triu, batch, seq)
    xa = _mix_xattn(x2, hb, gate, o_f, o_b, kt, vm, conv_w[0].astype(F32), row2(conv_norm[0]), gmat,
                    row2(gla_norm[0]), w_out[0].astype(BF16), row2(xa_norm[0]),
                    w_xq[0].astype(BF16), w_xo[0].astype(BF16), seq)
    out = _mlp(xa, row2(mlp_norm[0]), w_up[0].astype(BF16), w_down[0].astype(BF16),
               row2(final_norm))
    return out.reshape(batch, seq, d)
```

```python
import functools

import jax
import jax.numpy as jnp
import numpy as np
from jax import lax
from jax.experimental import pallas as pl
from jax.experimental.pallas import tpu as pltpu

F32 = jnp.float32
BF16 = jnp.bfloat16

EPS = 1e-6
LOG2_E = 1.4426950408889634
D_MODEL = 1024
CONV_WIDTH = 512
CONV_GROUP = 64
GLA_HEADS = 4
GLA_DK = 64
GLA_DV = 128
GLA_K_TOTAL = GLA_HEADS * GLA_DK
GLA_V_TOTAL = GLA_HEADS * GLA_DV
GLA_LOWRANK = 16
GLA_GATE_NORM = 16.0
GLA_CHUNK = 64
N_MEM = 256
XA_HEADS = 4
XA_HEAD_DIM = D_MODEL // XA_HEADS
D_FF = 4 * D_MODEL

LANES = 128
BF16_SUBLANES = 16
VMEM_LIMIT_BYTES = 56 * 1024 * 1024

TOKEN_TILE = 1024
MIX_TILE = 1024
MLP_TILE = 1024
MLP_STAGE_BYTES = 2 * 1024 * 1024
XA_ROW_SPLIT = 4
WEIGHT_PREP_ROWS = 128
GLA_BLOCK = 1024
CUMSUM_BLOCK = 256
FF_CHUNK = 4096


def _dot(a, b):
    return jnp.dot(a, b, preferred_element_type=F32)


def _dot_nt(a, b):
    return lax.dot_general(a, b, (((1,), (1,)), ((), ())), preferred_element_type=F32)


def _dot_tn(a, b):
    return lax.dot_general(a, b, (((0,), (0,)), ((), ())), preferred_element_type=F32)


def _inv_rms(x):
    return lax.rsqrt(jnp.mean(x * x, axis=-1, keepdims=True) + EPS)


def _rms(x, gain):
    return x * _inv_rms(x) * gain


def _params(semantics):
    return pltpu.CompilerParams(dimension_semantics=semantics,
                                vmem_limit_bytes=VMEM_LIMIT_BYTES)


def _in_proj_kernel(x_ref, gain_ref, wt_ref, waf_ref, wab_ref, baf_ref, bab_ref,
                    hb_ref, qkv_ref, gate_ref, la_ref, wc_ref, wg_ref, wa_ref):
    cw = CONV_WIDTH
    o_q = 3 * cw
    o_k = o_q + GLA_K_TOTAL
    o_lr = o_q + 2 * GLA_K_TOTAL + 2 * GLA_V_TOTAL

    @pl.when(pl.program_id(0) == 0)
    def _():
        def convert(dst_ref, dst0, src0, rows, scale=None):
            for r in range(0, rows, WEIGHT_PREP_ROWS):
                nr = min(WEIGHT_PREP_ROWS, rows - r)
                blk = wt_ref[src0 + r:src0 + r + nr, :]
                if scale is not None:
                    blk = blk * scale
                dst_ref[dst0 + r:dst0 + r + nr, :] = blk.astype(BF16)

        convert(wc_ref, 0, cw, 2 * cw)
        convert(wc_ref, 2 * cw, 0, cw)
        wg_ref[0:LANES, :] = jnp.zeros((LANES, D_MODEL), BF16)
        convert(wg_ref, 0, o_lr, 2 * GLA_LOWRANK)
        convert(wg_ref, LANES, o_q, GLA_K_TOTAL, GLA_DK ** -0.5)
        convert(wg_ref, LANES + GLA_K_TOTAL, o_k, o_lr - o_k)
        wa_ref[...] = jnp.zeros(wa_ref.shape, BF16)
        wa_ref[0:GLA_LOWRANK, 0:GLA_K_TOTAL] = waf_ref[...].astype(BF16)
        wa_ref[GLA_LOWRANK:2 * GLA_LOWRANK, GLA_K_TOTAL:] = wab_ref[...].astype(BF16)

    x = x_ref[...]
    r = _inv_rms(x)
    xg = (x * gain_ref[...]).astype(BF16)
    qkv_cols = 2 * GLA_K_TOTAL + GLA_V_TOTAL
    zc = _dot_nt(xg, wc_ref[...])
    hb_ref[:, 0:cw] = (zc[:, 0:cw] * zc[:, cw:2 * cw] * (r * r)).astype(BF16)
    hb_ref[:, cw:2 * cw] = (zc[:, 2 * cw:3 * cw] * r).astype(BF16)
    zg = _dot_nt(xg, wg_ref[...]) * r
    qkv_ref[...] = zg[:, LANES:LANES + qkv_cols].astype(BF16)
    g = zg[:, LANES + qkv_cols:]
    gate_ref[...] = (g / (1.0 + jnp.exp(-g))).astype(BF16)
    codes = zg[:, 0:LANES].astype(BF16)
    z = _dot(codes, wa_ref[...]) + jnp.concatenate([baf_ref[...], bab_ref[...]], axis=1)
    l2a = ((jnp.minimum(z, 0.0) - jnp.log(1.0 + jnp.exp(-jnp.abs(z))))
           * (LOG2_E / GLA_GATE_NORM))
    la_ref[...] = l2a.astype(BF16)


def _in_proj(x2, gain, w_in_t, w_af, w_ab, b_af, b_ab):
    n = x2.shape[0]
    assert w_af.shape == w_ab.shape == (GLA_LOWRANK, GLA_K_TOTAL)
    assert b_af.shape == b_ab.shape == (1, GLA_K_TOTAL)
    const = lambda i: (0, 0)
    row = lambda i: (i, 0)
    conv_cols = 3 * CONV_WIDTH
    gla_cols = 2 * GLA_K_TOTAL + 2 * GLA_V_TOTAL
    assert w_in_t.shape == (conv_cols + gla_cols + 2 * GLA_LOWRANK, D_MODEL)
    outs = (2 * CONV_WIDTH, 2 * GLA_K_TOTAL + GLA_V_TOTAL, GLA_V_TOTAL, 2 * GLA_K_TOTAL)
    return pl.pallas_call(
        _in_proj_kernel,
        out_shape=tuple(jax.ShapeDtypeStruct((n, c), BF16) for c in outs),
        grid=(n // TOKEN_TILE,),
        in_specs=[pl.BlockSpec((TOKEN_TILE, D_MODEL), row),
                  pl.BlockSpec(gain.shape, const),
                  pl.BlockSpec(w_in_t.shape, const, pipeline_mode=pl.Buffered(1)),
                  pl.BlockSpec(w_af.shape, const),
                  pl.BlockSpec(w_ab.shape, const),
                  pl.BlockSpec(b_af.shape, const),
                  pl.BlockSpec(b_ab.shape, const)],
        out_specs=tuple(pl.BlockSpec((TOKEN_TILE, c), row) for c in outs),
        scratch_shapes=[pltpu.VMEM((conv_cols, D_MODEL), BF16),
                        pltpu.VMEM((LANES + gla_cols, D_MODEL), BF16),
                        pltpu.VMEM((LANES, 2 * GLA_K_TOTAL), BF16)],
        compiler_params=_params(("arbitrary",)),
        name="in_proj",
    )(x2, gain, w_in_t, w_af, w_ab, b_af, b_ab)


def _mem_kv_kernel(m_ref, gain_ref, w_ref, kt_ref, v_ref):
    h = _rms(m_ref[...], gain_ref[...]).astype(BF16)
    kv = _dot(h, w_ref[...])
    kt_ref[...] = kv[:, 0:D_MODEL].T.astype(BF16)
    v_ref[...] = kv[:, D_MODEL:].astype(BF16)


def _mem_kv(mem2, gain, w):
    n = mem2.shape[0]
    batch = n // N_MEM
    const = lambda i: (0, 0)
    row = lambda i: (i, 0)
    return pl.pallas_call(
        _mem_kv_kernel,
        out_shape=(jax.ShapeDtypeStruct((batch * D_MODEL, N_MEM), BF16),
                   jax.ShapeDtypeStruct((n, D_MODEL), BF16)),
        grid=(batch,),
        in_specs=[pl.BlockSpec((N_MEM, D_MODEL), row),
                  pl.BlockSpec(gain.shape, const),
                  pl.BlockSpec(w.shape, const)],
        out_specs=(pl.BlockSpec((D_MODEL, N_MEM), row),
                   pl.BlockSpec((N_MEM, D_MODEL), row)),
        compiler_params=_params(("arbitrary",)),
        name="mem_kv",
    )(mem2, gain, w)


def _gla_prepare(qkv_ref, la_ref, tri_ref, qt_s, kt_s, kh_s, dec_s, reverse):
    tri = tri_ref[...]
    for r in range(GLA_BLOCK // CUMSUM_BLOCK):
        b = _dot(tri, la_ref[r * CUMSUM_BLOCK:(r + 1) * CUMSUM_BLOCK, :])
        grow = jnp.exp2(b)
        shrink = jnp.exp2(-b)
        rows = slice(r * CUMSUM_BLOCK, (r + 1) * CUMSUM_BLOCK)
        qt_s[rows, :] = qkv_ref[rows, 0:GLA_K_TOTAL] * grow.astype(BF16)
        kt = qkv_ref[rows, GLA_K_TOTAL:2 * GLA_K_TOTAL] * shrink.astype(BF16)
        kt_s[rows, :] = kt
        for cc in range(CUMSUM_BLOCK // GLA_CHUNK):
            c = r * (CUMSUM_BLOCK // GLA_CHUNK) + cc
            last = cc * GLA_CHUNK + (0 if reverse else GLA_CHUNK - 1)
            dec = grow[last:last + 1, :]
            dec_s[c] = dec
            crow = slice(cc * GLA_CHUNK, (cc + 1) * GLA_CHUNK)
            kh_s[c * GLA_CHUNK:(c + 1) * GLA_CHUNK, :] = kt[crow] * dec.astype(BF16)


def _gla_chunk(c, qkv_ref, o_ref, qt_s, kt_s, kh_s, dec_s, st_ref, reverse):
    rows = slice(c * GLA_CHUNK, (c + 1) * GLA_CHUNK)
    lane = lax.broadcasted_iota(jnp.int32, (GLA_CHUNK, LANES), 1)
    row = lax.broadcasted_iota(jnp.int32, (GLA_CHUNK, LANES), 0)
    first_head = lane < GLA_DK
    src = lane & (GLA_DK - 1)
    keep = (src > row) if reverse else (src <= row)
    vlane = lax.broadcasted_iota(jnp.int32, (GLA_CHUNK, 2 * GLA_DV), 1)
    first_head_v = vlane < GLA_DV
    dec = dec_s[c]
    zero = jnp.zeros((), BF16)
    for p in range(GLA_HEADS // 2):
        ksl = slice(p * LANES, (p + 1) * LANES)
        osl = slice(p * 2 * GLA_DV, (p + 1) * 2 * GLA_DV)
        vsl = slice(2 * GLA_K_TOTAL + p * 2 * GLA_DV, 2 * GLA_K_TOTAL + (p + 1) * 2 * GLA_DV)
        qc = qt_s[rows, ksl]
        kc = kt_s[rows, ksl]
        khc = kh_s[rows, ksl]
        vc = qkv_ref[rows, vsl]
        kbd = jnp.concatenate([jnp.where(first_head, kc, zero),
                               jnp.where(first_head, zero, kc)], axis=0)
        khbd = jnp.concatenate([jnp.where(first_head, khc, zero),
                                jnp.where(first_head, zero, khc)], axis=0)
        vbd = jnp.concatenate([jnp.where(first_head_v, vc, zero),
                               jnp.where(first_head_v, zero, vc)], axis=0)
        scores = _dot_nt(qc, kbd)
        scores = jnp.where(keep, scores, 0.0).astype(BF16)
        st = st_ref[p]
        o = _dot(scores, vbd) + _dot_nt(qc, st.astype(BF16))
        st_ref[p] = st * dec[:, ksl] + _dot_tn(vbd, khbd)
        o_ref[rows, osl] = o.astype(o_ref.dtype)


def _gla_kernel(qkv_f_ref, la_f_ref, qkv_b_ref, la_b_ref, tril_ref, triu_ref, of_ref, ob_ref,
                qt_f, kt_f, kh_f, dec_f, st_f, qt_b, kt_b, kh_b, dec_b, st_b):
    nchunk = GLA_BLOCK // GLA_CHUNK

    @pl.when(pl.program_id(1) == 0)
    def _():
        st_f[...] = jnp.zeros_like(st_f)
        st_b[...] = jnp.zeros_like(st_b)

    _gla_prepare(qkv_f_ref, la_f_ref, tril_ref, qt_f, kt_f, kh_f, dec_f, False)
    _gla_prepare(qkv_b_ref, la_b_ref, triu_ref, qt_b, kt_b, kh_b, dec_b, True)

    for c in range(nchunk):
        _gla_chunk(c, qkv_f_ref, of_ref, qt_f, kt_f, kh_f, dec_f, st_f, False)
        _gla_chunk(nchunk - 1 - c, qkv_b_ref, ob_ref, qt_b, kt_b, kh_b, dec_b, st_b, True)


def _gla(qkv, la, tril, triu, batch, seq):
    n = qkv.shape[0]
    nb = seq // GLA_BLOCK
    nchunk = GLA_BLOCK // GLA_CHUNK
    qkv_cols = 2 * GLA_K_TOTAL + GLA_V_TOTAL
    fwd = lambda b, i: (b * nb + i, 0)
    bwd = lambda b, i: (b * nb + nb - 1 - i, 0)
    bwd_la = lambda b, i: (b * nb + nb - 1 - i, 1)
    const = lambda b, i: (0, 0)
    dir_scratch = [pltpu.VMEM((GLA_BLOCK, GLA_K_TOTAL), BF16),
                   pltpu.VMEM((GLA_BLOCK, GLA_K_TOTAL), BF16),
                   pltpu.VMEM((GLA_BLOCK, GLA_K_TOTAL), BF16),
                   pltpu.VMEM((nchunk, 1, GLA_K_TOTAL), F32),
                   pltpu.VMEM((GLA_HEADS // 2, 2 * GLA_DV, LANES), F32)]
    return pl.pallas_call(
        _gla_kernel,
        out_shape=(jax.ShapeDtypeStruct((n, GLA_V_TOTAL), BF16),
                   jax.ShapeDtypeStruct((n, GLA_V_TOTAL), BF16)),
        grid=(batch, nb),
        in_specs=[pl.BlockSpec((GLA_BLOCK, qkv_cols), fwd),
                  pl.BlockSpec((GLA_BLOCK, GLA_K_TOTAL), fwd),
                  pl.BlockSpec((GLA_BLOCK, qkv_cols), bwd),
                  pl.BlockSpec((GLA_BLOCK, GLA_K_TOTAL), bwd_la),
                  pl.BlockSpec(tril.shape, const),
                  pl.BlockSpec(triu.shape, const)],
        out_specs=(pl.BlockSpec((GLA_BLOCK, GLA_V_TOTAL), fwd),
                   pl.BlockSpec((GLA_BLOCK, GLA_V_TOTAL), bwd)),
        scratch_shapes=dir_scratch + dir_scratch,
        compiler_params=_params(("arbitrary", "arbitrary")),
        name="gla",
    )(qkv, la, qkv, la, tril, triu)


def _mix_xattn_kernel(tiles_per_seq, x_ref, hb_ref, prev_ref, next_ref, gate_ref, of_ref, ob_ref,
                      kt_ref, vm_ref, convw_ref, convg_ref, gmat_ref, glag_ref, wout_ref,
                      xag_ref, wxq_ref, wxo_ref, o_ref):
    tm = x_ref.shape[0]
    t = pl.program_id(0) % tiles_per_seq
    has_prev = (t > 0).astype(F32)
    has_next = (t < tiles_per_seq - 1).astype(F32)

    h = hb_ref[:, 0:CONV_WIDTH].astype(F32)
    last = BF16_SUBLANES - 1
    h_before = prev_ref[last:last + 1, :].astype(F32) * has_prev
    h_after = next_ref[0:1, :].astype(F32) * has_next
    rowid = lax.broadcasted_iota(jnp.int32, (tm, CONV_WIDTH), 0)
    h_m1 = jnp.where(rowid == 0, h_before, pltpu.roll(h, 1, 0))
    h_p1 = jnp.where(rowid == tm - 1, h_after, pltpu.roll(h, tm - 1, 0))
    y = convw_ref[0:1, :] * h_m1 + convw_ref[1:2, :] * h + convw_ref[2:3, :] * h_p1
    y = hb_ref[:, CONV_WIDTH:2 * CONV_WIDTH].astype(F32) * y
    ms = _dot((y * y).astype(BF16), gmat_ref[...])
    y = y * lax.rsqrt(ms + EPS) * convg_ref[...]

    o = of_ref[...].astype(F32) + ob_ref[...].astype(F32)
    heads = []
    for hd in range(GLA_HEADS):
        oh = o[:, hd * GLA_DV:(hd + 1) * GLA_DV]
        heads.append(_rms(oh, glag_ref[...]))
    o = jnp.concatenate(heads, axis=1) * gate_ref[...].astype(F32)

    mixed = jnp.concatenate([y.astype(BF16), o.astype(BF16)], axis=1)
    x1 = x_ref[...] + _dot(mixed, wout_ref[...])

    r = _inv_rms(x1) * (XA_HEAD_DIM ** -0.5 * LOG2_E)
    q = (_dot((x1 * xag_ref[...]).astype(BF16), wxq_ref[...]) * r).astype(BF16)
    half_rows = tm // XA_ROW_SPLIT
    halves = []
    for top in range(0, tm, half_rows):
        rows = slice(top, top + half_rows)
        outs = []
        for hd in range(XA_HEADS):
            sl = slice(hd * XA_HEAD_DIM, (hd + 1) * XA_HEAD_DIM)
            s = _dot(q[rows, sl], kt_ref[sl, :])
            s = s - jnp.max(s, axis=-1, keepdims=True)
            e = jnp.exp2(s)
            p = e / jnp.sum(e, axis=-1, keepdims=True)
            outs.append(_dot(p.astype(BF16), vm_ref[:, sl]).astype(BF16))
        halves.append(jnp.concatenate(outs, axis=1))
    att = jnp.concatenate(halves, axis=0)
    o_ref[...] = x1 + _dot(att, wxo_ref[...])


def _mix_xattn(x2, hb, gate, o_f, o_b, kt, vm, convw, convg, gmat, glag, wout, xag, wxq, wxo, seq):
    n = x2.shape[0]
    tm = MIX_TILE
    tiles_per_seq = seq // tm
    halo_per_tile = tm // BF16_SUBLANES
    n_halo = n // BF16_SUBLANES
    const = lambda i: (0, 0)
    row = lambda i: (i, 0)
    return pl.pallas_call(
        functools.partial(_mix_xattn_kernel, tiles_per_seq),
        out_shape=jax.ShapeDtypeStruct((n, D_MODEL), F32),
        grid=(n // tm,),
        in_specs=[pl.BlockSpec((tm, D_MODEL), row),
                  pl.BlockSpec((tm, 2 * CONV_WIDTH), row),
                  pl.BlockSpec((BF16_SUBLANES, CONV_WIDTH),
                               lambda i: (jnp.maximum(i * halo_per_tile - 1, 0), 0)),
                  pl.BlockSpec((BF16_SUBLANES, CONV_WIDTH),
                               lambda i: (jnp.minimum((i + 1) * halo_per_tile, n_halo - 1), 0)),
                  pl.BlockSpec((tm, GLA_V_TOTAL), row),
                  pl.BlockSpec((tm, GLA_V_TOTAL), row),
                  pl.BlockSpec((tm, GLA_V_TOTAL), row),
                  pl.BlockSpec((D_MODEL, N_MEM), lambda i: (i // tiles_per_seq, 0)),
                  pl.BlockSpec((N_MEM, D_MODEL), lambda i: (i // tiles_per_seq, 0)),
                  pl.BlockSpec(convw.shape, const),
                  pl.BlockSpec(convg.shape, const),
                  pl.BlockSpec(gmat.shape, const),
                  pl.BlockSpec(glag.shape, const),
                  pl.BlockSpec(wout.shape, const),
                  pl.BlockSpec(xag.shape, const),
                  pl.BlockSpec(wxq.shape, const),
                  pl.BlockSpec(wxo.shape, const)],
        out_specs=pl.BlockSpec((tm, D_MODEL), row),
        compiler_params=_params(("arbitrary",)),
        name="mix_xattn",
    )(x2, hb, hb, hb, gate, o_f, o_b, kt, vm, convw, convg, gmat, glag, wout, xag, wxq, wxo)


def _stream_to_bf16(src_hbm, dst_ref, stage_ref, sem_ref):
    chunk_rows = stage_ref.shape[1]
    n_chunks = src_hbm.shape[0] // chunk_rows

    def chunk_copy(k):
        return pltpu.make_async_copy(src_hbm.at[pl.ds(k * chunk_rows, chunk_rows), :],
                                     stage_ref.at[k % 2], sem_ref.at[k % 2])

    chunk_copy(0).start()
    for k in range(n_chunks):
        if k + 1 < n_chunks:
            chunk_copy(k + 1).start()
        chunk_copy(k).wait()
        dst_ref[k * chunk_rows:(k + 1) * chunk_rows, :] = stage_ref[k % 2].astype(BF16)


def _mlp_kernel(x_ref, gain_ref, wu_hbm, wd_hbm, fgain_ref, o_ref,
                wu_ref, wd_ref, stage_u_ref, stage_d_ref, sem_ref):
    @pl.when(pl.program_id(0) == 0)
    def _():
        _stream_to_bf16(wu_hbm, wu_ref, stage_u_ref, sem_ref)
        _stream_to_bf16(wd_hbm, wd_ref, stage_d_ref, sem_ref)

    x = x_ref[...]
    r = _inv_rms(x)
    xg = (x * gain_ref[...]).astype(BF16)
    acc = None
    for c in range(D_FF // FF_CHUNK):
        sl = slice(c * FF_CHUNK, (c + 1) * FF_CHUNK)
        u = jnp.maximum(_dot(xg, wu_ref[:, sl]), 0.0)
        part = _dot((u * u).astype(BF16), wd_ref[sl, :])
        acc = part if acc is None else acc + part
    o_ref[...] = _rms(x + acc * (r * r), fgain_ref[...])


def _mlp(x2, gain, wu, wd, fgain):
    n = x2.shape[0]
    const = lambda i: (0, 0)
    row = lambda i: (i, 0)
    assert wu.shape == (D_MODEL, D_FF) and wd.shape == (D_FF, D_MODEL)
    assert wu.dtype == wd.dtype == F32
    stage_elems = MLP_STAGE_BYTES // 4
    stage_u_rows = stage_elems // D_FF
    stage_d_rows = stage_elems // D_MODEL
    assert D_MODEL % stage_u_rows == 0 and D_FF % stage_d_rows == 0
    return pl.pallas_call(
        _mlp_kernel,
        out_shape=jax.ShapeDtypeStruct((n, D_MODEL), F32),
        grid=(n // MLP_TILE,),
        in_specs=[pl.BlockSpec((MLP_TILE, D_MODEL), row),
                  pl.BlockSpec(gain.shape, const),
                  pl.BlockSpec(memory_space=pl.ANY),
                  pl.BlockSpec(memory_space=pl.ANY),
                  pl.BlockSpec(fgain.shape, const)],
        out_specs=pl.BlockSpec((MLP_TILE, D_MODEL), row),
        scratch_shapes=[pltpu.VMEM(wu.shape, BF16),
                        pltpu.VMEM(wd.shape, BF16),
                        pltpu.VMEM((2, stage_u_rows, D_FF), F32),
                        pltpu.VMEM((2, stage_d_rows, D_MODEL), F32),
                        pltpu.SemaphoreType.DMA((2,))],
        compiler_params=_params(("arbitrary",)),
        name="mlp",
    )(x2, gain, wu, wd, fgain)


def _block_diag_ones(n, block):
    i = np.arange(n)
    return (i[:, None] // block) == (i[None, :] // block)


def _constant(values):
    return jnp.asarray(np.asarray(values, np.float32).astype(BF16))


def kernel(x, mem, mix_norm, w_in, conv_w, conv_norm, w_af, b_af, w_ab, b_ab, gla_norm, w_out, xa_norm, mem_norm, w_xq, w_xkv, w_xo, mlp_norm, w_up, w_down, final_norm):
    batch, seq, d = x.shape
    assert d == D_MODEL and seq % GLA_BLOCK == 0 and seq % TOKEN_TILE == 0
    assert seq % MIX_TILE == 0 and (batch * seq) % MLP_TILE == 0
    assert mix_norm.shape[0] == 1, "single-layer block"
    n = batch * seq
    x2 = x.reshape(n, d)
    row2 = lambda v: v.reshape(1, -1).astype(F32)

    cw = CONV_WIDTH
    ci = np.arange(CUMSUM_BLOCK)
    same_chunk = _block_diag_ones(CUMSUM_BLOCK, GLA_CHUNK)
    tril = _constant(same_chunk & (ci[None, :] <= ci[:, None]))
    triu = _constant(same_chunk & (ci[None, :] >= ci[:, None]))
    gmat = _constant(_block_diag_ones(cw, CONV_GROUP) / CONV_GROUP)

    hb, qkv, gate, la = _in_proj(x2, row2(mix_norm[0]), w_in[0].T, w_af[0], w_ab[0],
                                 row2(b_af[0]), row2(b_ab[0]))
    kt, vm = _mem_kv(mem.reshape(batch * N_MEM, d), row2(mem_norm[0]), w_xkv[0].astype(BF16))
    o_f, o_b = _gla(qkv, la, tril, triu, batch, seq)
    xa = _mix_xattn(x2, hb, gate, o_f, o_b, kt, vm, conv_w[0].astype(F32), row2(conv_norm[0]), gmat,
                    row2(gla_norm[0]), w_out[0].astype(BF16), row2(xa_norm[0]),
                    w_xq[0].astype(BF16), w_xo[0].astype(BF16), seq)
    out = _mlp(xa, row2(mlp_norm[0]), w_up[0], w_down[0],
               row2(final_norm))
    return out.reshape(batch, seq, d)
```

```python
import functools

import jax
import jax.numpy as jnp
import numpy as np
from jax import lax
from jax.experimental import pallas as pl
from jax.experimental.pallas import tpu as pltpu

F32 = jnp.float32
BF16 = jnp.bfloat16

EPS = 1e-6
LOG2_E = 1.4426950408889634
D_MODEL = 1024
CONV_WIDTH = 512
CONV_GROUP = 64
GLA_HEADS = 4
GLA_DK = 64
GLA_DV = 128
GLA_K_TOTAL = GLA_HEADS * GLA_DK
GLA_V_TOTAL = GLA_HEADS * GLA_DV
GLA_LOWRANK = 16
GLA_GATE_NORM = 16.0
GLA_CHUNK = 64
N_MEM = 256
XA_HEADS = 4
XA_HEAD_DIM = D_MODEL // XA_HEADS
D_FF = 4 * D_MODEL

LANES = 128
BF16_SUBLANES = 16
VMEM_LIMIT_BYTES = 56 * 1024 * 1024

TOKEN_TILE = 1024
MIX_TILE = 1024
MLP_TILE = 1024
XA_ROW_SPLIT = 4
WEIGHT_PREP_ROWS = 128
GLA_BLOCK = 1024
CUMSUM_BLOCK = 256
FF_CHUNK = 4096


def _dot(a, b):
    return jnp.dot(a, b, preferred_element_type=F32)


def _dot_nt(a, b):
    return lax.dot_general(a, b, (((1,), (1,)), ((), ())), preferred_element_type=F32)


def _dot_tn(a, b):
    return lax.dot_general(a, b, (((0,), (0,)), ((), ())), preferred_element_type=F32)


def _inv_rms(x):
    return lax.rsqrt(jnp.mean(x * x, axis=-1, keepdims=True) + EPS)


def _rms(x, gain):
    return x * _inv_rms(x) * gain


def _params(semantics):
    return pltpu.CompilerParams(dimension_semantics=semantics,
                                vmem_limit_bytes=VMEM_LIMIT_BYTES)


def _cast_row_blocks(weights, steps):
    shapes = []
    for w in weights:
        assert w.dtype == F32 and w.shape[0] % (steps * BF16_SUBLANES) == 0
        shapes.append((w.shape[0] // steps, w.shape[1]))
    return shapes


def _cast_weights(src_refs, dst_refs):
    for src_ref, dst_ref in zip(src_refs, dst_refs):
        dst_ref[...] = src_ref[...].astype(BF16)


def _in_proj_kernel(n_cast, x_ref, gain_ref, wt_ref, waf_ref, wab_ref, baf_ref, bab_ref, *refs):
    cast_src = refs[:n_cast]
    hb_ref, qkv_ref, gate_ref, la_ref = refs[n_cast:n_cast + 4]
    cast_dst = refs[n_cast + 4:2 * n_cast + 4]
    wc_ref, wg_ref, wa_ref = refs[2 * n_cast + 4:]
    _cast_weights(cast_src, cast_dst)
    cw = CONV_WIDTH
    o_q = 3 * cw
    o_k = o_q + GLA_K_TOTAL
    o_lr = o_q + 2 * GLA_K_TOTAL + 2 * GLA_V_TOTAL

    @pl.when(pl.program_id(0) == 0)
    def _():
        def convert(dst_ref, dst0, src0, rows, scale=None):
            for r in range(0, rows, WEIGHT_PREP_ROWS):
                nr = min(WEIGHT_PREP_ROWS, rows - r)
                blk = wt_ref[src0 + r:src0 + r + nr, :]
                if scale is not None:
                    blk = blk * scale
                dst_ref[dst0 + r:dst0 + r + nr, :] = blk.astype(BF16)

        convert(wc_ref, 0, cw, 2 * cw)
        convert(wc_ref, 2 * cw, 0, cw)
        wg_ref[0:LANES, :] = jnp.zeros((LANES, D_MODEL), BF16)
        convert(wg_ref, 0, o_lr, 2 * GLA_LOWRANK)
        convert(wg_ref, LANES, o_q, GLA_K_TOTAL, GLA_DK ** -0.5)
        convert(wg_ref, LANES + GLA_K_TOTAL, o_k, o_lr - o_k)
        wa_ref[...] = jnp.zeros(wa_ref.shape, BF16)
        wa_ref[0:GLA_LOWRANK, 0:GLA_K_TOTAL] = waf_ref[...].astype(BF16)
        wa_ref[GLA_LOWRANK:2 * GLA_LOWRANK, GLA_K_TOTAL:] = wab_ref[...].astype(BF16)

    x = x_ref[...]
    r = _inv_rms(x)
    xg = (x * gain_ref[...]).astype(BF16)
    qkv_cols = 2 * GLA_K_TOTAL + GLA_V_TOTAL
    zc = _dot_nt(xg, wc_ref[...])
    hb_ref[:, 0:cw] = (zc[:, 0:cw] * zc[:, cw:2 * cw] * (r * r)).astype(BF16)
    hb_ref[:, cw:2 * cw] = (zc[:, 2 * cw:3 * cw] * r).astype(BF16)
    zg = _dot_nt(xg, wg_ref[...]) * r
    qkv_ref[...] = zg[:, LANES:LANES + qkv_cols].astype(BF16)
    g = zg[:, LANES + qkv_cols:]
    gate_ref[...] = (g / (1.0 + jnp.exp(-g))).astype(BF16)
    codes = zg[:, 0:LANES].astype(BF16)
    z = _dot(codes, wa_ref[...]) + jnp.concatenate([baf_ref[...], bab_ref[...]], axis=1)
    l2a = ((jnp.minimum(z, 0.0) - jnp.log(1.0 + jnp.exp(-jnp.abs(z))))
           * (LOG2_E / GLA_GATE_NORM))
    la_ref[...] = l2a.astype(BF16)


def _in_proj(x2, gain, w_in_t, w_af, w_ab, b_af, b_ab, later_weights):
    n = x2.shape[0]
    steps = n // TOKEN_TILE
    cast_blocks = _cast_row_blocks(later_weights, steps)
    assert w_af.shape == w_ab.shape == (GLA_LOWRANK, GLA_K_TOTAL)
    assert b_af.shape == b_ab.shape == (1, GLA_K_TOTAL)
    const = lambda i: (0, 0)
    row = lambda i: (i, 0)
    conv_cols = 3 * CONV_WIDTH
    gla_cols = 2 * GLA_K_TOTAL + 2 * GLA_V_TOTAL
    assert w_in_t.shape == (conv_cols + gla_cols + 2 * GLA_LOWRANK, D_MODEL)
    outs = (2 * CONV_WIDTH, 2 * GLA_K_TOTAL + GLA_V_TOTAL, GLA_V_TOTAL, 2 * GLA_K_TOTAL)
    results = pl.pallas_call(
        functools.partial(_in_proj_kernel, len(later_weights)),
        out_shape=(tuple(jax.ShapeDtypeStruct((n, c), BF16) for c in outs)
                   + tuple(jax.ShapeDtypeStruct(w.shape, BF16) for w in later_weights)),
        grid=(steps,),
        in_specs=[pl.BlockSpec((TOKEN_TILE, D_MODEL), row),
                  pl.BlockSpec(gain.shape, const),
                  pl.BlockSpec(w_in_t.shape, const, pipeline_mode=pl.Buffered(1)),
                  pl.BlockSpec(w_af.shape, const),
                  pl.BlockSpec(w_ab.shape, const),
                  pl.BlockSpec(b_af.shape, const),
                  pl.BlockSpec(b_ab.shape, const)] + [pl.BlockSpec(b, row) for b in cast_blocks],
        out_specs=(tuple(pl.BlockSpec((TOKEN_TILE, c), row) for c in outs)
                   + tuple(pl.BlockSpec(b, row) for b in cast_blocks)),
        scratch_shapes=[pltpu.VMEM((conv_cols, D_MODEL), BF16),
                        pltpu.VMEM((LANES + gla_cols, D_MODEL), BF16),
                        pltpu.VMEM((LANES, 2 * GLA_K_TOTAL), BF16)],
        compiler_params=_params(("arbitrary",)),
        name="in_proj",
    )(x2, gain, w_in_t, w_af, w_ab, b_af, b_ab, *later_weights)
    return results[:len(outs)], results[len(outs):]


def _mem_kv_kernel(m_ref, gain_ref, w_ref, kt_ref, v_ref):
    h = _rms(m_ref[...], gain_ref[...]).astype(BF16)
    kv = _dot(h, w_ref[...])
    kt_ref[...] = kv[:, 0:D_MODEL].T.astype(BF16)
    v_ref[...] = kv[:, D_MODEL:].astype(BF16)


def _mem_kv(mem2, gain, w):
    n = mem2.shape[0]
    batch = n // N_MEM
    const = lambda i: (0, 0)
    row = lambda i: (i, 0)
    return pl.pallas_call(
        _mem_kv_kernel,
        out_shape=(jax.ShapeDtypeStruct((batch * D_MODEL, N_MEM), BF16),
                   jax.ShapeDtypeStruct((n, D_MODEL), BF16)),
        grid=(batch,),
        in_specs=[pl.BlockSpec((N_MEM, D_MODEL), row),
                  pl.BlockSpec(gain.shape, const),
                  pl.BlockSpec(w.shape, const)],
        out_specs=(pl.BlockSpec((D_MODEL, N_MEM), row),
                   pl.BlockSpec((N_MEM, D_MODEL), row)),
        compiler_params=_params(("arbitrary",)),
        name="mem_kv",
    )(mem2, gain, w)


def _gla_prepare(qkv_ref, la_ref, tri_ref, qt_s, kt_s, kh_s, dec_s, reverse):
    tri = tri_ref[...]
    for r in range(GLA_BLOCK // CUMSUM_BLOCK):
        b = _dot(tri, la_ref[r * CUMSUM_BLOCK:(r + 1) * CUMSUM_BLOCK, :])
        grow = jnp.exp2(b)
        shrink = jnp.exp2(-b)
        rows = slice(r * CUMSUM_BLOCK, (r + 1) * CUMSUM_BLOCK)
        qt_s[rows, :] = qkv_ref[rows, 0:GLA_K_TOTAL] * grow.astype(BF16)
        kt = qkv_ref[rows, GLA_K_TOTAL:2 * GLA_K_TOTAL] * shrink.astype(BF16)
        kt_s[rows, :] = kt
        for cc in range(CUMSUM_BLOCK // GLA_CHUNK):
            c = r * (CUMSUM_BLOCK // GLA_CHUNK) + cc
            last = cc * GLA_CHUNK + (0 if reverse else GLA_CHUNK - 1)
            dec = grow[last:last + 1, :]
            dec_s[c] = dec
            crow = slice(cc * GLA_CHUNK, (cc + 1) * GLA_CHUNK)
            kh_s[c * GLA_CHUNK:(c + 1) * GLA_CHUNK, :] = kt[crow] * dec.astype(BF16)


def _gla_chunk(c, qkv_ref, o_ref, qt_s, kt_s, kh_s, dec_s, st_ref, reverse):
    rows = slice(c * GLA_CHUNK, (c + 1) * GLA_CHUNK)
    lane = lax.broadcasted_iota(jnp.int32, (GLA_CHUNK, LANES), 1)
    row = lax.broadcasted_iota(jnp.int32, (GLA_CHUNK, LANES), 0)
    first_head = lane < GLA_DK
    src = lane & (GLA_DK - 1)
    keep = (src > row) if reverse else (src <= row)
    vlane = lax.broadcasted_iota(jnp.int32, (GLA_CHUNK, 2 * GLA_DV), 1)
    first_head_v = vlane < GLA_DV
    dec = dec_s[c]
    zero = jnp.zeros((), BF16)
    for p in range(GLA_HEADS // 2):
        ksl = slice(p * LANES, (p + 1) * LANES)
        osl = slice(p * 2 * GLA_DV, (p + 1) * 2 * GLA_DV)
        vsl = slice(2 * GLA_K_TOTAL + p * 2 * GLA_DV, 2 * GLA_K_TOTAL + (p + 1) * 2 * GLA_DV)
        qc = qt_s[rows, ksl]
        kc = kt_s[rows, ksl]
        khc = kh_s[rows, ksl]
        vc = qkv_ref[rows, vsl]
        kbd = jnp.concatenate([jnp.where(first_head, kc, zero),
                               jnp.where(first_head, zero, kc)], axis=0)
        khbd = jnp.concatenate([jnp.where(first_head, khc, zero),
                                jnp.where(first_head, zero, khc)], axis=0)
        vbd = jnp.concatenate([jnp.where(first_head_v, vc, zero),
                               jnp.where(first_head_v, zero, vc)], axis=0)
        scores = _dot_nt(qc, kbd)
        scores = jnp.where(keep, scores, 0.0).astype(BF16)
        st = st_ref[p]
        o = _dot(scores, vbd) + _dot_nt(qc, st.astype(BF16))
        st_ref[p] = st * dec[:, ksl] + _dot_tn(vbd, khbd)
        o_ref[rows, osl] = o.astype(o_ref.dtype)


def _gla_kernel(n_cast, qkv_f_ref, la_f_ref, qkv_b_ref, la_b_ref, tril_ref, triu_ref, *refs):
    cast_src = refs[:n_cast]
    of_ref, ob_ref = refs[n_cast:n_cast + 2]
    cast_dst = refs[n_cast + 2:2 * n_cast + 2]
    qt_f, kt_f, kh_f, dec_f, st_f, qt_b, kt_b, kh_b, dec_b, st_b = refs[2 * n_cast + 2:]
    nchunk = GLA_BLOCK // GLA_CHUNK
    _cast_weights(cast_src, cast_dst)

    @pl.when(pl.program_id(1) == 0)
    def _():
        st_f[...] = jnp.zeros_like(st_f)
        st_b[...] = jnp.zeros_like(st_b)

    _gla_prepare(qkv_f_ref, la_f_ref, tril_ref, qt_f, kt_f, kh_f, dec_f, False)
    _gla_prepare(qkv_b_ref, la_b_ref, triu_ref, qt_b, kt_b, kh_b, dec_b, True)

    for c in range(nchunk):
        _gla_chunk(c, qkv_f_ref, of_ref, qt_f, kt_f, kh_f, dec_f, st_f, False)
        _gla_chunk(nchunk - 1 - c, qkv_b_ref, ob_ref, qt_b, kt_b, kh_b, dec_b, st_b, True)


def _gla(qkv, la, tril, triu, batch, seq, later_weights):
    n = qkv.shape[0]
    nb = seq // GLA_BLOCK
    cast_blocks = _cast_row_blocks(later_weights, batch * nb)
    step_row = lambda b, i: (b * nb + i, 0)
    nchunk = GLA_BLOCK // GLA_CHUNK
    qkv_cols = 2 * GLA_K_TOTAL + GLA_V_TOTAL
    fwd = lambda b, i: (b * nb + i, 0)
    bwd = lambda b, i: (b * nb + nb - 1 - i, 0)
    bwd_la = lambda b, i: (b * nb + nb - 1 - i, 1)
    const = lambda b, i: (0, 0)
    dir_scratch = [pltpu.VMEM((GLA_BLOCK, GLA_K_TOTAL), BF16),
                   pltpu.VMEM((GLA_BLOCK, GLA_K_TOTAL), BF16),
                   pltpu.VMEM((GLA_BLOCK, GLA_K_TOTAL), BF16),
                   pltpu.VMEM((nchunk, 1, GLA_K_TOTAL), F32),
                   pltpu.VMEM((GLA_HEADS // 2, 2 * GLA_DV, LANES), F32)]
    results = pl.pallas_call(
        functools.partial(_gla_kernel, len(later_weights)),
        out_shape=((jax.ShapeDtypeStruct((n, GLA_V_TOTAL), BF16),
                    jax.ShapeDtypeStruct((n, GLA_V_TOTAL), BF16))
                   + tuple(jax.ShapeDtypeStruct(w.shape, BF16) for w in later_weights)),
        grid=(batch, nb),
        in_specs=[pl.BlockSpec((GLA_BLOCK, qkv_cols), fwd),
                  pl.BlockSpec((GLA_BLOCK, GLA_K_TOTAL), fwd),
                  pl.BlockSpec((GLA_BLOCK, qkv_cols), bwd),
                  pl.BlockSpec((GLA_BLOCK, GLA_K_TOTAL), bwd_la),
                  pl.BlockSpec(tril.shape, const),
                  pl.BlockSpec(triu.shape, const)] + [pl.BlockSpec(b, step_row) for b in cast_blocks],
        out_specs=((pl.BlockSpec((GLA_BLOCK, GLA_V_TOTAL), fwd),
                    pl.BlockSpec((GLA_BLOCK, GLA_V_TOTAL), bwd))
                   + tuple(pl.BlockSpec(b, step_row) for b in cast_blocks)),
        scratch_shapes=dir_scratch + dir_scratch,
        compiler_params=_params(("arbitrary", "arbitrary")),
        name="gla",
    )(qkv, la, qkv, la, tril, triu, *later_weights)
    return results[:2], results[2:]


def _mix_xattn_kernel(tiles_per_seq, x_ref, hb_ref, prev_ref, next_ref, gate_ref, of_ref, ob_ref,
                      kt_ref, vm_ref, convw_ref, convg_ref, gmat_ref, glag_ref, wout_ref,
                      xag_ref, wxq_ref, wxo_ref, o_ref):
    tm = x_ref.shape[0]
    t = pl.program_id(0) % tiles_per_seq
    has_prev = (t > 0).astype(F32)
    has_next = (t < tiles_per_seq - 1).astype(F32)

    h = hb_ref[:, 0:CONV_WIDTH].astype(F32)
    last = BF16_SUBLANES - 1
    h_before = prev_ref[last:last + 1, :].astype(F32) * has_prev
    h_after = next_ref[0:1, :].astype(F32) * has_next
    rowid = lax.broadcasted_iota(jnp.int32, (tm, CONV_WIDTH), 0)
    h_m1 = jnp.where(rowid == 0, h_before, pltpu.roll(h, 1, 0))
    h_p1 = jnp.where(rowid == tm - 1, h_after, pltpu.roll(h, tm - 1, 0))
    y = convw_ref[0:1, :] * h_m1 + convw_ref[1:2, :] * h + convw_ref[2:3, :] * h_p1
    y = hb_ref[:, CONV_WIDTH:2 * CONV_WIDTH].astype(F32) * y
    ms = _dot((y * y).astype(BF16), gmat_ref[...])
    y = y * lax.rsqrt(ms + EPS) * convg_ref[...]

    o = of_ref[...].astype(F32) + ob_ref[...].astype(F32)
    heads = []
    for hd in range(GLA_HEADS):
        oh = o[:, hd * GLA_DV:(hd + 1) * GLA_DV]
        heads.append(_rms(oh, glag_ref[...]))
    o = jnp.concatenate(heads, axis=1) * gate_ref[...].astype(F32)

    mixed = jnp.concatenate([y.astype(BF16), o.astype(BF16)], axis=1)
    x1 = x_ref[...] + _dot(mixed, wout_ref[...])

    r = _inv_rms(x1) * (XA_HEAD_DIM ** -0.5 * LOG2_E)
    q = (_dot((x1 * xag_ref[...]).astype(BF16), wxq_ref[...]) * r).astype(BF16)
    half_rows = tm // XA_ROW_SPLIT
    halves = []
    for top in range(0, tm, half_rows):
        rows = slice(top, top + half_rows)
        outs = []
        for hd in range(XA_HEADS):
            sl = slice(hd * XA_HEAD_DIM, (hd + 1) * XA_HEAD_DIM)
            s = _dot(q[rows, sl], kt_ref[sl, :])
            s = s - jnp.max(s, axis=-1, keepdims=True)
            e = jnp.exp2(s)
            p = e / jnp.sum(e, axis=-1, keepdims=True)
            outs.append(_dot(p.astype(BF16), vm_ref[:, sl]).astype(BF16))
        halves.append(jnp.concatenate(outs, axis=1))
    att = jnp.concatenate(halves, axis=0)
    o_ref[...] = x1 + _dot(att, wxo_ref[...])


def _mix_xattn(x2, hb, gate, o_f, o_b, kt, vm, convw, convg, gmat, glag, wout, xag, wxq, wxo, seq):
    n = x2.shape[0]
    tm = MIX_TILE
    tiles_per_seq = seq // tm
    halo_per_tile = tm // BF16_SUBLANES
    n_halo = n // BF16_SUBLANES
    const = lambda i: (0, 0)
    row = lambda i: (i, 0)
    return pl.pallas_call(
        functools.partial(_mix_xattn_kernel, tiles_per_seq),
        out_shape=jax.ShapeDtypeStruct((n, D_MODEL), F32),
        grid=(n // tm,),
        in_specs=[pl.BlockSpec((tm, D_MODEL), row),
                  pl.BlockSpec((tm, 2 * CONV_WIDTH), row),
                  pl.BlockSpec((BF16_SUBLANES, CONV_WIDTH),
                               lambda i: (jnp.maximum(i * halo_per_tile - 1, 0), 0)),
                  pl.BlockSpec((BF16_SUBLANES, CONV_WIDTH),
                               lambda i: (jnp.minimum((i + 1) * halo_per_tile, n_halo - 1), 0)),
                  pl.BlockSpec((tm, GLA_V_TOTAL), row),
                  pl.BlockSpec((tm, GLA_V_TOTAL), row),
                  pl.BlockSpec((tm, GLA_V_TOTAL), row),
                  pl.BlockSpec((D_MODEL, N_MEM), lambda i: (i // tiles_per_seq, 0)),
                  pl.BlockSpec((N_MEM, D_MODEL), lambda i: (i // tiles_per_seq, 0)),
                  pl.BlockSpec(convw.shape, const),
                  pl.BlockSpec(convg.shape, const),
                  pl.BlockSpec(gmat.shape, const),
                  pl.BlockSpec(glag.shape, const),
                  pl.BlockSpec(wout.shape, const),
                  pl.BlockSpec(xag.shape, const),
                  pl.BlockSpec(wxq.shape, const),
                  pl.BlockSpec(wxo.shape, const)],
        out_specs=pl.BlockSpec((tm, D_MODEL), row),
        compiler_params=_params(("arbitrary",)),
        name="mix_xattn",
    )(x2, hb, hb, hb, gate, o_f, o_b, kt, vm, convw, convg, gmat, glag, wout, xag, wxq, wxo)


def _mlp_kernel(x_ref, gain_ref, wu_ref, wd_ref, fgain_ref, o_ref):
    x = x_ref[...]
    r = _inv_rms(x)
    xg = (x * gain_ref[...]).astype(BF16)
    acc = None
    for c in range(D_FF // FF_CHUNK):
        sl = slice(c * FF_CHUNK, (c + 1) * FF_CHUNK)
        u = jnp.maximum(_dot(xg, wu_ref[:, sl]), 0.0)
        part = _dot((u * u).astype(BF16), wd_ref[sl, :])
        acc = part if acc is None else acc + part
    o_ref[...] = _rms(x + acc * (r * r), fgain_ref[...])


def _mlp(x2, gain, wu, wd, fgain):
    n = x2.shape[0]
    const = lambda i: (0, 0)
    row = lambda i: (i, 0)
    resident = lambda shape: pl.BlockSpec(shape, const, pipeline_mode=pl.Buffered(1))
    return pl.pallas_call(
        _mlp_kernel,
        out_shape=jax.ShapeDtypeStruct((n, D_MODEL), F32),
        grid=(n // MLP_TILE,),
        in_specs=[pl.BlockSpec((MLP_TILE, D_MODEL), row),
                  pl.BlockSpec(gain.shape, const),
                  resident(wu.shape),
                  resident(wd.shape),
                  pl.BlockSpec(fgain.shape, const)],
        out_specs=pl.BlockSpec((MLP_TILE, D_MODEL), row),
        compiler_params=_params(("arbitrary",)),
        name="mlp",
    )(x2, gain, wu, wd, fgain)


def _block_diag_ones(n, block):
    i = np.arange(n)
    return (i[:, None] // block) == (i[None, :] // block)


def _constant(values):
    return jnp.asarray(np.asarray(values, np.float32).astype(BF16))


def kernel(x, mem, mix_norm, w_in, conv_w, conv_norm, w_af, b_af, w_ab, b_ab, gla_norm, w_out, xa_norm, mem_norm, w_xq, w_xkv, w_xo, mlp_norm, w_up, w_down, final_norm):
    batch, seq, d = x.shape
    assert d == D_MODEL and seq % GLA_BLOCK == 0 and seq % TOKEN_TILE == 0
    assert seq % MIX_TILE == 0 and (batch * seq) % MLP_TILE == 0
    assert mix_norm.shape[0] == 1, "single-layer block"
    n = batch * seq
    x2 = x.reshape(n, d)
    row2 = lambda v: v.reshape(1, -1).astype(F32)

    cw = CONV_WIDTH
    ci = np.arange(CUMSUM_BLOCK)
    same_chunk = _block_diag_ones(CUMSUM_BLOCK, GLA_CHUNK)
    tril = _constant(same_chunk & (ci[None, :] <= ci[:, None]))
    triu = _constant(same_chunk & (ci[None, :] >= ci[:, None]))
    gmat = _constant(_block_diag_ones(cw, CONV_GROUP) / CONV_GROUP)

    (hb, qkv, gate, la), (wxkv, wout, wxq, wxo) = _in_proj(
        x2, row2(mix_norm[0]), w_in[0].T, w_af[0], w_ab[0], row2(b_af[0]), row2(b_ab[0]),
        (w_xkv[0], w_out[0], w_xq[0], w_xo[0]))
    kt, vm = _mem_kv(mem.reshape(batch * N_MEM, d), row2(mem_norm[0]), wxkv)
    (o_f, o_b), (wup, wdown) = _gla(qkv, la, tril, triu, batch, seq, (w_up[0], w_down[0]))
    xa = _mix_xattn(x2, hb, gate, o_f, o_b, kt, vm, conv_w[0].astype(F32), row2(conv_norm[0]), gmat,
                    row2(gla_norm[0]), wout, row2(xa_norm[0]), wxq, wxo, seq)
    out = _mlp(xa, row2(mlp_norm[0]), wup, wdown, row2(final_norm))
    return out.reshape(batch, seq, d)
```

```python
import functools

import jax
import jax.numpy as jnp
import numpy as np
from jax import lax
from jax.experimental import pallas as pl
from jax.experimental.pallas import tpu as pltpu

F32 = jnp.float32
BF16 = jnp.bfloat16

EPS = 1e-6
LOG2_E = 1.4426950408889634
D_MODEL = 1024
CONV_WIDTH = 512
CONV_GROUP = 64
GLA_HEADS = 4
GLA_DK = 64
GLA_DV = 128
GLA_K_TOTAL = GLA_HEADS * GLA_DK
GLA_V_TOTAL = GLA_HEADS * GLA_DV
GLA_LOWRANK = 16
GLA_GATE_NORM = 16.0
GLA_CHUNK = 64
N_MEM = 256
XA_HEADS = 4
XA_HEAD_DIM = D_MODEL // XA_HEADS
D_FF = 4 * D_MODEL

LANES = 128
BF16_SUBLANES = 16
VMEM_LIMIT_BYTES = 56 * 1024 * 1024

TOKEN_TILE = 1024
MIX_TILE = 1024
MLP_TILE = 1024
XA_ROW_SPLIT = 4
WEIGHT_PREP_ROWS = 128
GLA_BLOCK = 1024
CUMSUM_BLOCK = 256
FF_CHUNK = 4096


def _dot(a, b):
    return jnp.dot(a, b, preferred_element_type=F32)


def _dot_nt(a, b):
    return lax.dot_general(a, b, (((1,), (1,)), ((), ())), preferred_element_type=F32)


def _dot_tn(a, b):
    return lax.dot_general(a, b, (((0,), (0,)), ((), ())), preferred_element_type=F32)


def _inv_rms(x):
    return lax.rsqrt(jnp.mean(x * x, axis=-1, keepdims=True) + EPS)


def _rms(x, gain):
    return x * _inv_rms(x) * gain


def _params(semantics):
    return pltpu.CompilerParams(dimension_semantics=semantics,
                                vmem_limit_bytes=VMEM_LIMIT_BYTES)


def _cast_row_blocks(weights, steps):
    shapes = []
    for w in weights:
        assert w.dtype == F32 and w.shape[0] % (steps * BF16_SUBLANES) == 0
        shapes.append((w.shape[0] // steps, w.shape[1]))
    return shapes


def _cast_weights(src_refs, dst_refs, piece=0, n_pieces=1):
    for src_ref, dst_ref in zip(src_refs, dst_refs):
        slabs = min(n_pieces, src_ref.shape[1] // LANES)
        assert n_pieces % slabs == 0 and src_ref.shape[1] % (slabs * LANES) == 0
        every = n_pieces // slabs
        if piece % every == 0:
            cols = src_ref.shape[1] // slabs
            c0 = (piece // every) * cols
            dst_ref[:, c0:c0 + cols] = src_ref[:, c0:c0 + cols].astype(BF16)


def _in_proj_kernel(n_cast, x_ref, gain_ref, wt_ref, waf_ref, wab_ref, baf_ref, bab_ref, *refs):
    cast_src = refs[:n_cast]
    hb_ref, qkv_ref, gate_ref, la_ref = refs[n_cast:n_cast + 4]
    cast_dst = refs[n_cast + 4:2 * n_cast + 4]
    wc_ref, wg_ref, wa_ref = refs[2 * n_cast + 4:]
    cw = CONV_WIDTH
    o_q = 3 * cw
    o_k = o_q + GLA_K_TOTAL
    o_lr = o_q + 2 * GLA_K_TOTAL + 2 * GLA_V_TOTAL

    @pl.when(pl.program_id(0) == 0)
    def _():
        def convert(dst_ref, dst0, src0, rows, scale=None):
            for r in range(0, rows, WEIGHT_PREP_ROWS):
                nr = min(WEIGHT_PREP_ROWS, rows - r)
                blk = wt_ref[src0 + r:src0 + r + nr, :]
                if scale is not None:
                    blk = blk * scale
                dst_ref[dst0 + r:dst0 + r + nr, :] = blk.astype(BF16)

        convert(wc_ref, 0, cw, 2 * cw)
        convert(wc_ref, 2 * cw, 0, cw)
        wg_ref[0:LANES, :] = jnp.zeros((LANES, D_MODEL), BF16)
        convert(wg_ref, 0, o_lr, 2 * GLA_LOWRANK)
        convert(wg_ref, LANES, o_q, GLA_K_TOTAL, GLA_DK ** -0.5)
        convert(wg_ref, LANES + GLA_K_TOTAL, o_k, o_lr - o_k)
        wa_ref[...] = jnp.zeros(wa_ref.shape, BF16)
        wa_ref[0:GLA_LOWRANK, 0:GLA_K_TOTAL] = waf_ref[...].astype(BF16)
        wa_ref[GLA_LOWRANK:2 * GLA_LOWRANK, GLA_K_TOTAL:] = wab_ref[...].astype(BF16)

    x = x_ref[...]
    r = _inv_rms(x)
    xg = (x * gain_ref[...]).astype(BF16)
    qkv_cols = 2 * GLA_K_TOTAL + GLA_V_TOTAL
    zc = _dot_nt(xg, wc_ref[...])
    hb_ref[:, 0:cw] = (zc[:, 0:cw] * zc[:, cw:2 * cw] * (r * r)).astype(BF16)
    hb_ref[:, cw:2 * cw] = (zc[:, 2 * cw:3 * cw] * r).astype(BF16)
    _cast_weights(cast_src, cast_dst)
    zg = _dot_nt(xg, wg_ref[...]) * r
    qkv_ref[...] = zg[:, LANES:LANES + qkv_cols].astype(BF16)
    g = zg[:, LANES + qkv_cols:]
    gate_ref[...] = (g / (1.0 + jnp.exp(-g))).astype(BF16)
    codes = zg[:, 0:LANES].astype(BF16)
    z = _dot(codes, wa_ref[...]) + jnp.concatenate([baf_ref[...], bab_ref[...]], axis=1)
    l2a = ((jnp.minimum(z, 0.0) - jnp.log(1.0 + jnp.exp(-jnp.abs(z))))
           * (LOG2_E / GLA_GATE_NORM))
    la_ref[...] = l2a.astype(BF16)


def _in_proj(x2, gain, w_in_t, w_af, w_ab, b_af, b_ab, later_weights):
    n = x2.shape[0]
    steps = n // TOKEN_TILE
    cast_blocks = _cast_row_blocks(later_weights, steps)
    assert w_af.shape == w_ab.shape == (GLA_LOWRANK, GLA_K_TOTAL)
    assert b_af.shape == b_ab.shape == (1, GLA_K_TOTAL)
    const = lambda i: (0, 0)
    row = lambda i: (i, 0)
    conv_cols = 3 * CONV_WIDTH
    gla_cols = 2 * GLA_K_TOTAL + 2 * GLA_V_TOTAL
    assert w_in_t.shape == (conv_cols + gla_cols + 2 * GLA_LOWRANK, D_MODEL)
    outs = (2 * CONV_WIDTH, 2 * GLA_K_TOTAL + GLA_V_TOTAL, GLA_V_TOTAL, 2 * GLA_K_TOTAL)
    results = pl.pallas_call(
        functools.partial(_in_proj_kernel, len(later_weights)),
        out_shape=(tuple(jax.ShapeDtypeStruct((n, c), BF16) for c in outs)
                   + tuple(jax.ShapeDtypeStruct(w.shape, BF16) for w in later_weights)),
        grid=(steps,),
        in_specs=[pl.BlockSpec((TOKEN_TILE, D_MODEL), row),
                  pl.BlockSpec(gain.shape, const),
                  pl.BlockSpec(w_in_t.shape, const, pipeline_mode=pl.Buffered(1)),
                  pl.BlockSpec(w_af.shape, const),
                  pl.BlockSpec(w_ab.shape, const),
                  pl.BlockSpec(b_af.shape, const),
                  pl.BlockSpec(b_ab.shape, const)] + [pl.BlockSpec(b, row) for b in cast_blocks],
        out_specs=(tuple(pl.BlockSpec((TOKEN_TILE, c), row) for c in outs)
                   + tuple(pl.BlockSpec(b, row) for b in cast_blocks)),
        scratch_shapes=[pltpu.VMEM((conv_cols, D_MODEL), BF16),
                        pltpu.VMEM((LANES + gla_cols, D_MODEL), BF16),
                        pltpu.VMEM((LANES, 2 * GLA_K_TOTAL), BF16)],
        compiler_params=_params(("arbitrary",)),
        name="in_proj",
    )(x2, gain, w_in_t, w_af, w_ab, b_af, b_ab, *later_weights)
    return results[:len(outs)], results[len(outs):]


def _mem_kv_kernel(m_ref, gain_ref, w_ref, kt_ref, v_ref):
    h = _rms(m_ref[...], gain_ref[...]).astype(BF16)
    kv = _dot(h, w_ref[...])
    kt_ref[...] = kv[:, 0:D_MODEL].T.astype(BF16)
    v_ref[...] = kv[:, D_MODEL:].astype(BF16)


def _mem_kv(mem2, gain, w):
    n = mem2.shape[0]
    batch = n // N_MEM
    const = lambda i: (0, 0)
    row = lambda i: (i, 0)
    return pl.pallas_call(
        _mem_kv_kernel,
        out_shape=(jax.ShapeDtypeStruct((batch * D_MODEL, N_MEM), BF16),
                   jax.ShapeDtypeStruct((n, D_MODEL), BF16)),
        grid=(batch,),
        in_specs=[pl.BlockSpec((N_MEM, D_MODEL), row),
                  pl.BlockSpec(gain.shape, const),
                  pl.BlockSpec(w.shape, const)],
        out_specs=(pl.BlockSpec((D_MODEL, N_MEM), row),
                   pl.BlockSpec((N_MEM, D_MODEL), row)),
        compiler_params=_params(("arbitrary",)),
        name="mem_kv",
    )(mem2, gain, w)


def _gla_prepare(qkv_ref, la_ref, tri_ref, qt_s, kt_s, kh_s, dec_s, reverse):
    tri = tri_ref[...]
    for r in range(GLA_BLOCK // CUMSUM_BLOCK):
        b = _dot(tri, la_ref[r * CUMSUM_BLOCK:(r + 1) * CUMSUM_BLOCK, :])
        grow = jnp.exp2(b)
        shrink = jnp.exp2(-b)
        rows = slice(r * CUMSUM_BLOCK, (r + 1) * CUMSUM_BLOCK)
        qt_s[rows, :] = qkv_ref[rows, 0:GLA_K_TOTAL] * grow.astype(BF16)
        kt = qkv_ref[rows, GLA_K_TOTAL:2 * GLA_K_TOTAL] * shrink.astype(BF16)
        kt_s[rows, :] = kt
        for cc in range(CUMSUM_BLOCK // GLA_CHUNK):
            c = r * (CUMSUM_BLOCK // GLA_CHUNK) + cc
            last = cc * GLA_CHUNK + (0 if reverse else GLA_CHUNK - 1)
            dec = grow[last:last + 1, :]
            dec_s[c] = dec
            crow = slice(cc * GLA_CHUNK, (cc + 1) * GLA_CHUNK)
            kh_s[c * GLA_CHUNK:(c + 1) * GLA_CHUNK, :] = kt[crow] * dec.astype(BF16)


def _gla_chunk(c, qkv_ref, o_ref, qt_s, kt_s, kh_s, dec_s, st_ref, reverse):
    rows = slice(c * GLA_CHUNK, (c + 1) * GLA_CHUNK)
    lane = lax.broadcasted_iota(jnp.int32, (GLA_CHUNK, LANES), 1)
    row = lax.broadcasted_iota(jnp.int32, (GLA_CHUNK, LANES), 0)
    first_head = lane < GLA_DK
    src = lane & (GLA_DK - 1)
    keep = (src > row) if reverse else (src <= row)
    vlane = lax.broadcasted_iota(jnp.int32, (GLA_CHUNK, 2 * GLA_DV), 1)
    first_head_v = vlane < GLA_DV
    dec = dec_s[c]
    zero = jnp.zeros((), BF16)
    for p in range(GLA_HEADS // 2):
        ksl = slice(p * LANES, (p + 1) * LANES)
        osl = slice(p * 2 * GLA_DV, (p + 1) * 2 * GLA_DV)
        vsl = slice(2 * GLA_K_TOTAL + p * 2 * GLA_DV, 2 * GLA_K_TOTAL + (p + 1) * 2 * GLA_DV)
        qc = qt_s[rows, ksl]
        kc = kt_s[rows, ksl]
        khc = kh_s[rows, ksl]
        vc = qkv_ref[rows, vsl]
        kbd = jnp.concatenate([jnp.where(first_head, kc, zero),
                               jnp.where(first_head, zero, kc)], axis=0)
        khbd = jnp.concatenate([jnp.where(first_head, khc, zero),
                                jnp.where(first_head, zero, khc)], axis=0)
        vbd = jnp.concatenate([jnp.where(first_head_v, vc, zero),
                               jnp.where(first_head_v, zero, vc)], axis=0)
        scores = _dot_nt(qc, kbd)
        scores = jnp.where(keep, scores, 0.0).astype(BF16)
        st = st_ref[p]
        o = _dot(scores, vbd) + _dot_nt(qc, st.astype(BF16))
        st_ref[p] = st * dec[:, ksl] + _dot_tn(vbd, khbd)
        o_ref[rows, osl] = o.astype(o_ref.dtype)


def _gla_kernel(n_cast, qkv_f_ref, la_f_ref, qkv_b_ref, la_b_ref, tril_ref, triu_ref, *refs):
    cast_src = refs[:n_cast]
    of_ref, ob_ref = refs[n_cast:n_cast + 2]
    cast_dst = refs[n_cast + 2:2 * n_cast + 2]
    qt_f, kt_f, kh_f, dec_f, st_f, qt_b, kt_b, kh_b, dec_b, st_b = refs[2 * n_cast + 2:]
    nchunk = GLA_BLOCK // GLA_CHUNK

    @pl.when(pl.program_id(1) == 0)
    def _():
        st_f[...] = jnp.zeros_like(st_f)
        st_b[...] = jnp.zeros_like(st_b)

    _gla_prepare(qkv_f_ref, la_f_ref, tril_ref, qt_f, kt_f, kh_f, dec_f, False)
    _gla_prepare(qkv_b_ref, la_b_ref, triu_ref, qt_b, kt_b, kh_b, dec_b, True)

    for c in range(nchunk):
        _gla_chunk(c, qkv_f_ref, of_ref, qt_f, kt_f, kh_f, dec_f, st_f, False)
        _gla_chunk(nchunk - 1 - c, qkv_b_ref, ob_ref, qt_b, kt_b, kh_b, dec_b, st_b, True)
        _cast_weights(cast_src, cast_dst, c, nchunk)


def _gla(qkv, la, tril, triu, batch, seq, later_weights):
    n = qkv.shape[0]
    nb = seq // GLA_BLOCK
    cast_blocks = _cast_row_blocks(later_weights, batch * nb)
    step_row = lambda b, i: (b * nb + i, 0)
    nchunk = GLA_BLOCK // GLA_CHUNK
    qkv_cols = 2 * GLA_K_TOTAL + GLA_V_TOTAL
    fwd = lambda b, i: (b * nb + i, 0)
    bwd = lambda b, i: (b * nb + nb - 1 - i, 0)
    bwd_la = lambda b, i: (b * nb + nb - 1 - i, 1)
    const = lambda b, i: (0, 0)
    dir_scratch = [pltpu.VMEM((GLA_BLOCK, GLA_K_TOTAL), BF16),
                   pltpu.VMEM((GLA_BLOCK, GLA_K_TOTAL), BF16),
                   pltpu.VMEM((GLA_BLOCK, GLA_K_TOTAL), BF16),
                   pltpu.VMEM((nchunk, 1, GLA_K_TOTAL), F32),
                   pltpu.VMEM((GLA_HEADS // 2, 2 * GLA_DV, LANES), F32)]
    results = pl.pallas_call(
        functools.partial(_gla_kernel, len(later_weights)),
        out_shape=((jax.ShapeDtypeStruct((n, GLA_V_TOTAL), BF16),
                    jax.ShapeDtypeStruct((n, GLA_V_TOTAL), BF16))
                   + tuple(jax.ShapeDtypeStruct(w.shape, BF16) for w in later_weights)),
        grid=(batch, nb),
        in_specs=[pl.BlockSpec((GLA_BLOCK, qkv_cols), fwd),
                  pl.BlockSpec((GLA_BLOCK, GLA_K_TOTAL), fwd),
                  pl.BlockSpec((GLA_BLOCK, qkv_cols), bwd),
                  pl.BlockSpec((GLA_BLOCK, GLA_K_TOTAL), bwd_la),
                  pl.BlockSpec(tril.shape, const),
                  pl.BlockSpec(triu.shape, const)] + [pl.BlockSpec(b, step_row) for b in cast_blocks],
        out_specs=((pl.BlockSpec((GLA_BLOCK, GLA_V_TOTAL), fwd),
                    pl.BlockSpec((GLA_BLOCK, GLA_V_TOTAL), bwd))
                   + tuple(pl.BlockSpec(b, step_row) for b in cast_blocks)),
        scratch_shapes=dir_scratch + dir_scratch,
        compiler_params=_params(("arbitrary", "arbitrary")),
        name="gla",
    )(qkv, la, qkv, la, tril, triu, *later_weights)
    return results[:2], results[2:]


def _mix_xattn_kernel(tiles_per_seq, x_ref, hb_ref, prev_ref, next_ref, gate_ref, of_ref, ob_ref,
                      kt_ref, vm_ref, convw_ref, convg_ref, gmat_ref, glag_ref, wout_ref,
                      xag_ref, wxq_ref, wxo_ref, o_ref):
    tm = x_ref.shape[0]
    t = pl.program_id(0) % tiles_per_seq
    has_prev = (t > 0).astype(F32)
    has_next = (t < tiles_per_seq - 1).astype(F32)

    h = hb_ref[:, 0:CONV_WIDTH].astype(F32)
    last = BF16_SUBLANES - 1
    h_before = prev_ref[last:last + 1, :].astype(F32) * has_prev
    h_after = next_ref[0:1, :].astype(F32) * has_next
    rowid = lax.broadcasted_iota(jnp.int32, (tm, CONV_WIDTH), 0)
    h_m1 = jnp.where(rowid == 0, h_before, pltpu.roll(h, 1, 0))
    h_p1 = jnp.where(rowid == tm - 1, h_after, pltpu.roll(h, tm - 1, 0))
    y = convw_ref[0:1, :] * h_m1 + convw_ref[1:2, :] * h + convw_ref[2:3, :] * h_p1
    y = hb_ref[:, CONV_WIDTH:2 * CONV_WIDTH].astype(F32) * y
    ms = _dot((y * y).astype(BF16), gmat_ref[...])
    y = y * lax.rsqrt(ms + EPS) * convg_ref[...]

    o = of_ref[...].astype(F32) + ob_ref[...].astype(F32)
    heads = []
    for hd in range(GLA_HEADS):
        oh = o[:, hd * GLA_DV:(hd + 1) * GLA_DV]
        heads.append(_rms(oh, glag_ref[...]))
    o = jnp.concatenate(heads, axis=1) * gate_ref[...].astype(F32)

    mixed = jnp.concatenate([y.astype(BF16), o.astype(BF16)], axis=1)
    x1 = x_ref[...] + _dot(mixed, wout_ref[...])

    r = _inv_rms(x1) * (XA_HEAD_DIM ** -0.5 * LOG2_E)
    q = (_dot((x1 * xag_ref[...]).astype(BF16), wxq_ref[...]) * r).astype(BF16)
    half_rows = tm // XA_ROW_SPLIT
    halves = []
    for top in range(0, tm, half_rows):
        rows = slice(top, top + half_rows)
        outs = []
        for hd in range(XA_HEADS):
            sl = slice(hd * XA_HEAD_DIM, (hd + 1) * XA_HEAD_DIM)
            s = _dot(q[rows, sl], kt_ref[sl, :])
            s = s - jnp.max(s, axis=-1, keepdims=True)
            e = jnp.exp2(s)
            p = e / jnp.sum(e, axis=-1, keepdims=True)
            outs.append(_dot(p.astype(BF16), vm_ref[:, sl]).astype(BF16))
        halves.append(jnp.concatenate(outs, axis=1))
    att = jnp.concatenate(halves, axis=0)
    o_ref[...] = x1 + _dot(att, wxo_ref[...])


def _mix_xattn(x2, hb, gate, o_f, o_b, kt, vm, convw, convg, gmat, glag, wout, xag, wxq, wxo, seq):
    n = x2.shape[0]
    tm = MIX_TILE
    tiles_per_seq = seq // tm
    halo_per_tile = tm // BF16_SUBLANES
    n_halo = n // BF16_SUBLANES
    const = lambda i: (0, 0)
    row = lambda i: (i, 0)
    return pl.pallas_call(
        functools.partial(_mix_xattn_kernel, tiles_per_seq),
        out_shape=jax.ShapeDtypeStruct((n, D_MODEL), F32),
        grid=(n // tm,),
        in_specs=[pl.BlockSpec((tm, D_MODEL), row),
                  pl.BlockSpec((tm, 2 * CONV_WIDTH), row),
                  pl.BlockSpec((BF16_SUBLANES, CONV_WIDTH),
                               lambda i: (jnp.maximum(i * halo_per_tile - 1, 0), 0)),
                  pl.BlockSpec((BF16_SUBLANES, CONV_WIDTH),
                               lambda i: (jnp.minimum((i + 1) * halo_per_tile, n_halo - 1), 0)),
                  pl.BlockSpec((tm, GLA_V_TOTAL), row),
                  pl.BlockSpec((tm, GLA_V_TOTAL), row),
                  pl.BlockSpec((tm, GLA_V_TOTAL), row),
                  pl.BlockSpec((D_MODEL, N_MEM), lambda i: (i // tiles_per_seq, 0)),
                  pl.BlockSpec((N_MEM, D_MODEL), lambda i: (i // tiles_per_seq, 0)),
                  pl.BlockSpec(convw.shape, const),
                  pl.BlockSpec(convg.shape, const),
                  pl.BlockSpec(gmat.shape, const),
                  pl.BlockSpec(glag.shape, const),
                  pl.BlockSpec(wout.shape, const),
                  pl.BlockSpec(xag.shape, const),
                  pl.BlockSpec(wxq.shape, const),
                  pl.BlockSpec(wxo.shape, const)],
        out_specs=pl.BlockSpec((tm, D_MODEL), row),
        compiler_params=_params(("arbitrary",)),
        name="mix_xattn",
    )(x2, hb, hb, hb, gate, o_f, o_b, kt, vm, convw, convg, gmat, glag, wout, xag, wxq, wxo)


def _mlp_kernel(x_ref, gain_ref, wu_ref, wd_ref, fgain_ref, o_ref):
    x = x_ref[...]
    r = _inv_rms(x)
    xg = (x * gain_ref[...]).astype(BF16)
    acc = None
    for c in range(D_FF // FF_CHUNK):
        sl = slice(c * FF_CHUNK, (c + 1) * FF_CHUNK)
        u = jnp.maximum(_dot(xg, wu_ref[:, sl]), 0.0)
        part = _dot((u * u).astype(BF16), wd_ref[sl, :])
        acc = part if acc is None else acc + part
    o_ref[...] = _rms(x + acc * (r * r), fgain_ref[...])


def _mlp(x2, gain, wu, wd, fgain):
    n = x2.shape[0]
    const = lambda i: (0, 0)
    row = lambda i: (i, 0)
    resident = lambda shape: pl.BlockSpec(shape, const, pipeline_mode=pl.Buffered(1))
    return pl.pallas_call(
        _mlp_kernel,
        out_shape=jax.ShapeDtypeStruct((n, D_MODEL), F32),
        grid=(n // MLP_TILE,),
        in_specs=[pl.BlockSpec((MLP_TILE, D_MODEL), row),
                  pl.BlockSpec(gain.shape, const),
                  resident(wu.shape),
                  resident(wd.shape),
                  pl.BlockSpec(fgain.shape, const)],
        out_specs=pl.BlockSpec((MLP_TILE, D_MODEL), row),
        compiler_params=_params(("arbitrary",)),
        name="mlp",
    )(x2, gain, wu, wd, fgain)


def _block_diag_ones(n, block):
    i = np.arange(n)
    return (i[:, None] // block) == (i[None, :] // block)


def _constant(values):
    return jnp.asarray(np.asarray(values, np.float32).astype(BF16))


def kernel(x, mem, mix_norm, w_in, conv_w, conv_norm, w_af, b_af, w_ab, b_ab, gla_norm, w_out, xa_norm, mem_norm, w_xq, w_xkv, w_xo, mlp_norm, w_up, w_down, final_norm):
    batch, seq, d = x.shape
    assert d == D_MODEL and seq % GLA_BLOCK == 0 and seq % TOKEN_TILE == 0
    assert seq % MIX_TILE == 0 and (batch * seq) % MLP_TILE == 0
    assert mix_norm.shape[0] == 1, "single-layer block"
    n = batch * seq
    x2 = x.reshape(n, d)
    row2 = lambda v: v.reshape(1, -1).astype(F32)

    cw = CONV_WIDTH
    ci = np.arange(CUMSUM_BLOCK)
    same_chunk = _block_diag_ones(CUMSUM_BLOCK, GLA_CHUNK)
    tril = _constant(same_chunk & (ci[None, :] <= ci[:, None]))
    triu = _constant(same_chunk & (ci[None, :] >= ci[:, None]))
    gmat = _constant(_block_diag_ones(cw, CONV_GROUP) / CONV_GROUP)

    (hb, qkv, gate, la), (wxkv, wout, wxq, wxo) = _in_proj(
        x2, row2(mix_norm[0]), w_in[0].T, w_af[0], w_ab[0], row2(b_af[0]), row2(b_ab[0]),
        (w_xkv[0], w_out[0], w_xq[0], w_xo[0]))
    kt, vm = _mem_kv(mem.reshape(batch * N_MEM, d), row2(mem_norm[0]), wxkv)
    (o_f, o_b), (wup, wdown) = _gla(qkv, la, tril, triu, batch, seq, (w_up[0], w_down[0]))
    xa = _mix_xattn(x2, hb, gate, o_f, o_b, kt, vm, conv_w[0].astype(F32), row2(conv_norm[0]), gmat,
                    row2(gla_norm[0]), wout, row2(xa_norm[0]), wxq, wxo, seq)
    out = _mlp(xa, row2(mlp_norm[0]), wup, wdown, row2(final_norm))
    return out.reshape(batch, seq, d)
```

```python
import functools

import jax
import jax.numpy as jnp
import numpy as np
from jax import lax
from jax.experimental import pallas as pl
from jax.experimental.pallas import tpu as pltpu

F32 = jnp.float32
BF16 = jnp.bfloat16

EPS = 1e-6
LOG2_E = 1.4426950408889634
D_MODEL = 1024
CONV_WIDTH = 512
CONV_GROUP = 64
GLA_HEADS = 4
GLA_DK = 64
GLA_DV = 128
GLA_K_TOTAL = GLA_HEADS * GLA_DK
GLA_V_TOTAL = GLA_HEADS * GLA_DV
GLA_LOWRANK = 16
GLA_GATE_NORM = 16.0
GLA_CHUNK = 64
N_MEM = 256
XA_HEADS = 4
XA_HEAD_DIM = D_MODEL // XA_HEADS
D_FF = 4 * D_MODEL

LANES = 128
BF16_SUBLANES = 16
VMEM_LIMIT_BYTES = 56 * 1024 * 1024

TOKEN_TILE = 1024
MIX_TILE = 1024
MLP_TILE = 1024
XA_ROW_SPLIT = 4
WEIGHT_PREP_ROWS = 128
GLA_BLOCK = 1024
CUMSUM_BLOCK = 256
FF_CHUNK = 4096


def _dot(a, b):
    return jnp.dot(a, b, preferred_element_type=F32)


def _dot_nt(a, b):
    return lax.dot_general(a, b, (((1,), (1,)), ((), ())), preferred_element_type=F32)


def _dot_tn(a, b):
    return lax.dot_general(a, b, (((0,), (0,)), ((), ())), preferred_element_type=F32)


def _inv_rms(x):
    return lax.rsqrt(jnp.mean(x * x, axis=-1, keepdims=True) + EPS)


def _rms(x, gain):
    return x * _inv_rms(x) * gain


def _params(semantics):
    return pltpu.CompilerParams(dimension_semantics=semantics,
                                vmem_limit_bytes=VMEM_LIMIT_BYTES)


def _cast_row_blocks(weights, steps):
    shapes = []
    for w in weights:
        assert w.dtype == F32 and w.shape[0] % (steps * BF16_SUBLANES) == 0
        shapes.append((w.shape[0] // steps, w.shape[1]))
    return shapes


def _cast_weights(src_refs, dst_refs, piece=0, n_pieces=1):
    for src_ref, dst_ref in zip(src_refs, dst_refs):
        slabs = min(n_pieces, src_ref.shape[1] // LANES)
        assert n_pieces % slabs == 0 and src_ref.shape[1] % (slabs * LANES) == 0
        every = n_pieces // slabs
        if piece % every == 0:
            cols = src_ref.shape[1] // slabs
            c0 = (piece // every) * cols
            dst_ref[:, c0:c0 + cols] = src_ref[:, c0:c0 + cols].astype(BF16)


def _in_proj_kernel(n_cast, x_ref, gain_ref, wt_ref, waf_ref, wab_ref, baf_ref, bab_ref, *refs):
    cast_src = refs[:n_cast]
    hb_ref, qkv_ref, gate_ref, la_ref = refs[n_cast:n_cast + 4]
    cast_dst = refs[n_cast + 4:2 * n_cast + 4]
    wc_ref, wg_ref, wa_ref = refs[2 * n_cast + 4:]
    cw = CONV_WIDTH
    o_q = 3 * cw
    o_k = o_q + GLA_K_TOTAL
    o_lr = o_q + 2 * GLA_K_TOTAL + 2 * GLA_V_TOTAL

    @pl.when(pl.program_id(0) == 0)
    def _():
        def convert(dst_ref, dst0, src0, rows, scale=None):
            for r in range(0, rows, WEIGHT_PREP_ROWS):
                nr = min(WEIGHT_PREP_ROWS, rows - r)
                blk = wt_ref[src0 + r:src0 + r + nr, :]
                if scale is not None:
                    blk = blk * scale
                dst_ref[dst0 + r:dst0 + r + nr, :] = blk.astype(BF16)

        convert(wc_ref, 0, cw, 2 * cw)
        convert(wc_ref, 2 * cw, 0, cw)
        wg_ref[0:LANES, :] = jnp.zeros((LANES, D_MODEL), BF16)
        convert(wg_ref, 0, o_lr, 2 * GLA_LOWRANK)
        convert(wg_ref, LANES, o_q, GLA_K_TOTAL, GLA_DK ** -0.5)
        convert(wg_ref, LANES + GLA_K_TOTAL, o_k, o_lr - o_k)
        wa_ref[...] = jnp.zeros(wa_ref.shape, BF16)
        wa_ref[0:GLA_LOWRANK, 0:GLA_K_TOTAL] = waf_ref[...].astype(BF16)
        wa_ref[GLA_LOWRANK:2 * GLA_LOWRANK, GLA_K_TOTAL:] = wab_ref[...].astype(BF16)

    x = x_ref[...]
    r = _inv_rms(x)
    xg = (x * gain_ref[...]).astype(BF16)
    qkv_cols = 2 * GLA_K_TOTAL + GLA_V_TOTAL
    zc = _dot_nt(xg, wc_ref[...])
    hb_ref[:, 0:cw] = (zc[:, 0:cw] * zc[:, cw:2 * cw] * (r * r)).astype(BF16)
    hb_ref[:, cw:2 * cw] = (zc[:, 2 * cw:3 * cw] * r).astype(BF16)
    _cast_weights(cast_src, cast_dst)
    zg = _dot_nt(xg, wg_ref[...]) * r
    qkv_ref[...] = zg[:, LANES:LANES + qkv_cols].astype(BF16)
    g = zg[:, LANES + qkv_cols:]
    gate_ref[...] = (g / (1.0 + jnp.exp(-g))).astype(BF16)
    codes = zg[:, 0:LANES].astype(BF16)
    z = _dot(codes, wa_ref[...]) + jnp.concatenate([baf_ref[...], bab_ref[...]], axis=1)
    l2a = ((jnp.minimum(z, 0.0) - jnp.log(1.0 + jnp.exp(-jnp.abs(z))))
           * (LOG2_E / GLA_GATE_NORM))
    la_ref[...] = l2a.astype(BF16)


def _in_proj(x2, gain, w_in_t, w_af, w_ab, b_af, b_ab, later_weights):
    n = x2.shape[0]
    steps = n // TOKEN_TILE
    cast_blocks = _cast_row_blocks(later_weights, steps)
    assert w_af.shape == w_ab.shape == (GLA_LOWRANK, GLA_K_TOTAL)
    assert b_af.shape == b_ab.shape == (1, GLA_K_TOTAL)
    const = lambda i: (0, 0)
    row = lambda i: (i, 0)
    conv_cols = 3 * CONV_WIDTH
    gla_cols = 2 * GLA_K_TOTAL + 2 * GLA_V_TOTAL
    assert w_in_t.shape == (conv_cols + gla_cols + 2 * GLA_LOWRANK, D_MODEL)
    outs = (2 * CONV_WIDTH, 2 * GLA_K_TOTAL + GLA_V_TOTAL, GLA_V_TOTAL, 2 * GLA_K_TOTAL)
    results = pl.pallas_call(
        functools.partial(_in_proj_kernel, len(later_weights)),
        out_shape=(tuple(jax.ShapeDtypeStruct((n, c), BF16) for c in outs)
                   + tuple(jax.ShapeDtypeStruct(w.shape, BF16) for w in later_weights)),
        grid=(steps,),
        in_specs=[pl.BlockSpec((TOKEN_TILE, D_MODEL), row),
                  pl.BlockSpec(gain.shape, const),
                  pl.BlockSpec(w_in_t.shape, const, pipeline_mode=pl.Buffered(1)),
                  pl.BlockSpec(w_af.shape, const),
                  pl.BlockSpec(w_ab.shape, const),
                  pl.BlockSpec(b_af.shape, const),
                  pl.BlockSpec(b_ab.shape, const)] + [pl.BlockSpec(b, row) for b in cast_blocks],
        out_specs=(tuple(pl.BlockSpec((TOKEN_TILE, c), row) for c in outs)
                   + tuple(pl.BlockSpec(b, row) for b in cast_blocks)),
        scratch_shapes=[pltpu.VMEM((conv_cols, D_MODEL), BF16),
                        pltpu.VMEM((LANES + gla_cols, D_MODEL), BF16),
                        pltpu.VMEM((LANES, 2 * GLA_K_TOTAL), BF16)],
        compiler_params=_params(("arbitrary",)),
        name="in_proj",
    )(x2, gain, w_in_t, w_af, w_ab, b_af, b_ab, *later_weights)
    return results[:len(outs)], results[len(outs):]


def _mem_kv_kernel(m_ref, gain_ref, w_ref, kt_ref, v_ref):
    h = _rms(m_ref[...], gain_ref[...]).astype(BF16)
    kv = _dot(h, w_ref[...])
    kt_ref[...] = kv[:, 0:D_MODEL].T.astype(BF16)
    v_ref[...] = kv[:, D_MODEL:].astype(BF16)


def _mem_kv(mem2, gain, w):
    n = mem2.shape[0]
    batch = n // N_MEM
    const = lambda i: (0, 0)
    row = lambda i: (i, 0)
    return pl.pallas_call(
        _mem_kv_kernel,
        out_shape=(jax.ShapeDtypeStruct((batch * D_MODEL, N_MEM), BF16),
                   jax.ShapeDtypeStruct((n, D_MODEL), BF16)),
        grid=(batch,),
        in_specs=[pl.BlockSpec((N_MEM, D_MODEL), row),
                  pl.BlockSpec(gain.shape, const),
                  pl.BlockSpec(w.shape, const)],
        out_specs=(pl.BlockSpec((D_MODEL, N_MEM), row),
                   pl.BlockSpec((N_MEM, D_MODEL), row)),
        compiler_params=_params(("arbitrary",)),
        name="mem_kv",
    )(mem2, gain, w)


def _gla_prepare(qkv_ref, la_ref, tri_ref, qt_s, kt_s, kh_s, dec_s, reverse):
    tri = tri_ref[...]
    for r in range(GLA_BLOCK // CUMSUM_BLOCK):
        b = _dot(tri, la_ref[r * CUMSUM_BLOCK:(r + 1) * CUMSUM_BLOCK, :])
        grow = jnp.exp2(b)
        shrink = jnp.exp2(-b)
        rows = slice(r * CUMSUM_BLOCK, (r + 1) * CUMSUM_BLOCK)
        qt_s[rows, :] = qkv_ref[rows, 0:GLA_K_TOTAL] * grow.astype(BF16)
        kt = qkv_ref[rows, GLA_K_TOTAL:2 * GLA_K_TOTAL] * shrink.astype(BF16)
        kt_s[rows, :] = kt
        for cc in range(CUMSUM_BLOCK // GLA_CHUNK):
            c = r * (CUMSUM_BLOCK // GLA_CHUNK) + cc
            last = cc * GLA_CHUNK + (0 if reverse else GLA_CHUNK - 1)
            dec = grow[last:last + 1, :]
            dec_s[c] = dec
            crow = slice(cc * GLA_CHUNK, (cc + 1) * GLA_CHUNK)
            kh_s[c * GLA_CHUNK:(c + 1) * GLA_CHUNK, :] = kt[crow] * dec.astype(BF16)


def _gla_chunk(c, qkv_ref, o_ref, qt_s, kt_s, kh_s, dec_s, st_ref, reverse):
    rows = slice(c * GLA_CHUNK, (c + 1) * GLA_CHUNK)
    lane = lax.broadcasted_iota(jnp.int32, (GLA_CHUNK, LANES), 1)
    row = lax.broadcasted_iota(jnp.int32, (GLA_CHUNK, LANES), 0)
    first_head = lane < GLA_DK
    src = lane & (GLA_DK - 1)
    keep = (src > row) if reverse else (src <= row)
    vlane = lax.broadcasted_iota(jnp.int32, (GLA_CHUNK, 2 * GLA_DV), 1)
    first_head_v = vlane < GLA_DV
    dec = dec_s[c]
    zero = jnp.zeros((), BF16)
    for p in range(GLA_HEADS // 2):
        ksl = slice(p * LANES, (p + 1) * LANES)
        osl = slice(p * 2 * GLA_DV, (p + 1) * 2 * GLA_DV)
        vsl = slice(2 * GLA_K_TOTAL + p * 2 * GLA_DV, 2 * GLA_K_TOTAL + (p + 1) * 2 * GLA_DV)
        qc = qt_s[rows, ksl]
        kc = kt_s[rows, ksl]
        khc = kh_s[rows, ksl]
        vc = qkv_ref[rows, vsl]
        kbd = jnp.concatenate([jnp.where(first_head, kc, zero),
                               jnp.where(first_head, zero, kc)], axis=0)
        khbd = jnp.concatenate([jnp.where(first_head, khc, zero),
                                jnp.where(first_head, zero, khc)], axis=0)
        vbd = jnp.concatenate([jnp.where(first_head_v, vc, zero),
                               jnp.where(first_head_v, zero, vc)], axis=0)
        scores = _dot_nt(qc, kbd)
        scores = jnp.where(keep, scores, 0.0).astype(BF16)
        st = st_ref[p]
        o = _dot(scores, vbd) + _dot_nt(qc, st.astype(BF16))
        st_ref[p] = st * dec[:, ksl] + _dot_tn(vbd, khbd)
        o_ref[rows, osl] = o.astype(o_ref.dtype)


def _gla_kernel(n_cast, qkv_f_ref, la_f_ref, qkv_b_ref, la_b_ref, tril_ref, triu_ref, *refs):
    cast_src = refs[:n_cast]
    of_ref, ob_ref = refs[n_cast:n_cast + 2]
    cast_dst = refs[n_cast + 2:2 * n_cast + 2]
    qt_f, kt_f, kh_f, dec_f, st_f, qt_b, kt_b, kh_b, dec_b, st_b = refs[2 * n_cast + 2:]
    nchunk = GLA_BLOCK // GLA_CHUNK

    @pl.when(pl.program_id(1) == 0)
    def _():
        st_f[...] = jnp.zeros_like(st_f)
        st_b[...] = jnp.zeros_like(st_b)

    _gla_prepare(qkv_f_ref, la_f_ref, tril_ref, qt_f, kt_f, kh_f, dec_f, False)
    _gla_prepare(qkv_b_ref, la_b_ref, triu_ref, qt_b, kt_b, kh_b, dec_b, True)

    for c in range(nchunk):
        _gla_chunk(c, qkv_f_ref, of_ref, qt_f, kt_f, kh_f, dec_f, st_f, False)
        _gla_chunk(nchunk - 1 - c, qkv_b_ref, ob_ref, qt_b, kt_b, kh_b, dec_b, st_b, True)
        _cast_weights(cast_src, cast_dst, c, nchunk)


def _gla(qkv, la, tril, triu, batch, seq, later_weights):
    n = qkv.shape[0]
    nb = seq // GLA_BLOCK
    cast_blocks = _cast_row_blocks(later_weights, batch * nb)
    step_row = lambda b, i: (b * nb + i, 0)
    nchunk = GLA_BLOCK // GLA_CHUNK
    qkv_cols = 2 * GLA_K_TOTAL + GLA_V_TOTAL
    fwd = lambda b, i: (b * nb + i, 0)
    bwd = lambda b, i: (b * nb + nb - 1 - i, 0)
    bwd_la = lambda b, i: (b * nb + nb - 1 - i, 1)
    const = lambda b, i: (0, 0)
    dir_scratch = [pltpu.VMEM((GLA_BLOCK, GLA_K_TOTAL), BF16),
                   pltpu.VMEM((GLA_BLOCK, GLA_K_TOTAL), BF16),
                   pltpu.VMEM((GLA_BLOCK, GLA_K_TOTAL), BF16),
                   pltpu.VMEM((nchunk, 1, GLA_K_TOTAL), F32),
                   pltpu.VMEM((GLA_HEADS // 2, 2 * GLA_DV, LANES), F32)]
    results = pl.pallas_call(
        functools.partial(_gla_kernel, len(later_weights)),
        out_shape=((jax.ShapeDtypeStruct((n, GLA_V_TOTAL), BF16),
                    jax.ShapeDtypeStruct((n, GLA_V_TOTAL), BF16))
                   + tuple(jax.ShapeDtypeStruct(w.shape, BF16) for w in later_weights)),
        grid=(batch, nb),
        in_specs=[pl.BlockSpec((GLA_BLOCK, qkv_cols), fwd),
                  pl.BlockSpec((GLA_BLOCK, GLA_K_TOTAL), fwd),
                  pl.BlockSpec((GLA_BLOCK, qkv_cols), bwd),
                  pl.BlockSpec((GLA_BLOCK, GLA_K_TOTAL), bwd_la),
                  pl.BlockSpec(tril.shape, const),
                  pl.BlockSpec(triu.shape, const)] + [pl.BlockSpec(b, step_row) for b in cast_blocks],
        out_specs=((pl.BlockSpec((GLA_BLOCK, GLA_V_TOTAL), fwd),
                    pl.BlockSpec((GLA_BLOCK, GLA_V_TOTAL), bwd))
                   + tuple(pl.BlockSpec(b, step_row) for b in cast_blocks)),
        scratch_shapes=dir_scratch + dir_scratch,
        compiler_params=_params(("arbitrary", "arbitrary")),
        name="gla",
    )(qkv, la, qkv, la, tril, triu, *later_weights)
    return results[:2], results[2:]


def _mix_xattn_kernel(tiles_per_seq, x_ref, hb_ref, prev_ref, next_ref, gate_ref, of_ref, ob_ref,
                      kt_ref, vm_ref, convw_ref, convg_ref, gmat_ref, glag_ref, wout_ref,
                      xag_ref, wxq_ref, wxo_ref, o_ref):
    tm = x_ref.shape[0]
    t = pl.program_id(0) % tiles_per_seq
    has_prev = (t > 0).astype(F32)
    has_next = (t < tiles_per_seq - 1).astype(F32)

    h = hb_ref[:, 0:CONV_WIDTH].astype(F32)
    last = BF16_SUBLANES - 1
    h_before = prev_ref[last:last + 1, :].astype(F32) * has_prev
    h_after = next_ref[0:1, :].astype(F32) * has_next
    rowid = lax.broadcasted_iota(jnp.int32, (tm, CONV_WIDTH), 0)
    h_m1 = jnp.where(rowid == 0, h_before, pltpu.roll(h, 1, 0))
    h_p1 = jnp.where(rowid == tm - 1, h_after, pltpu.roll(h, tm - 1, 0))
    y = convw_ref[0] * h_m1 + convw_ref[1] * h + convw_ref[2] * h_p1
    y = hb_ref[:, CONV_WIDTH:2 * CONV_WIDTH].astype(F32) * y
    ms = _dot((y * y).astype(BF16), gmat_ref[...])
    y = y * lax.rsqrt(ms + EPS) * convg_ref[...]

    o = of_ref[...].astype(F32) + ob_ref[...].astype(F32)
    heads = []
    for hd in range(GLA_HEADS):
        oh = o[:, hd * GLA_DV:(hd + 1) * GLA_DV]
        heads.append(_rms(oh, glag_ref[...]))
    o = jnp.concatenate(heads, axis=1) * gate_ref[...].astype(F32)

    mixed = jnp.concatenate([y.astype(BF16), o.astype(BF16)], axis=1)
    x1 = x_ref[...] + _dot(mixed, wout_ref[...])

    r = _inv_rms(x1) * (XA_HEAD_DIM ** -0.5 * LOG2_E)
    q = (_dot((x1 * xag_ref[...]).astype(BF16), wxq_ref[...]) * r).astype(BF16)
    half_rows = tm // XA_ROW_SPLIT
    halves = []
    for top in range(0, tm, half_rows):
        rows = slice(top, top + half_rows)
        outs = []
        for hd in range(XA_HEADS):
            sl = slice(hd * XA_HEAD_DIM, (hd + 1) * XA_HEAD_DIM)
            s = _dot(q[rows, sl], kt_ref[sl, :])
            s = s - jnp.max(s, axis=-1, keepdims=True)
            e = jnp.exp2(s)
            p = e / jnp.sum(e, axis=-1, keepdims=True)
            outs.append(_dot(p.astype(BF16), vm_ref[:, sl]).astype(BF16))
        halves.append(jnp.concatenate(outs, axis=1))
    att = jnp.concatenate(halves, axis=0)
    o_ref[...] = x1 + _dot(att, wxo_ref[...])


def _mix_xattn(x2, hb, gate, o_f, o_b, kt, vm, convw, convg, gmat, glag, wout, xag, wxq, wxo, seq):
    n = x2.shape[0]
    tm = MIX_TILE
    tiles_per_seq = seq // tm
    halo_per_tile = tm // BF16_SUBLANES
    n_halo = n // BF16_SUBLANES
    const = lambda i: (0, 0)
    row = lambda i: (i, 0)
    return pl.pallas_call(
        functools.partial(_mix_xattn_kernel, tiles_per_seq),
        out_shape=jax.ShapeDtypeStruct((n, D_MODEL), F32),
        grid=(n // tm,),
        in_specs=[pl.BlockSpec((tm, D_MODEL), row),
                  pl.BlockSpec((tm, 2 * CONV_WIDTH), row),
                  pl.BlockSpec((BF16_SUBLANES, CONV_WIDTH),
                               lambda i: (jnp.maximum(i * halo_per_tile - 1, 0), 0)),
                  pl.BlockSpec((BF16_SUBLANES, CONV_WIDTH),
                               lambda i: (jnp.minimum((i + 1) * halo_per_tile, n_halo - 1), 0)),
                  pl.BlockSpec((tm, GLA_V_TOTAL), row),
                  pl.BlockSpec((tm, GLA_V_TOTAL), row),
                  pl.BlockSpec((tm, GLA_V_TOTAL), row),
                  pl.BlockSpec((D_MODEL, N_MEM), lambda i: (i // tiles_per_seq, 0)),
                  pl.BlockSpec((N_MEM, D_MODEL), lambda i: (i // tiles_per_seq, 0)),
                  pl.BlockSpec(convw.shape, lambda i: (0, 0, 0)),
                  pl.BlockSpec(convg.shape, const),
                  pl.BlockSpec(gmat.shape, const),
                  pl.BlockSpec(glag.shape, const),
                  pl.BlockSpec(wout.shape, const),
                  pl.BlockSpec(xag.shape, const),
                  pl.BlockSpec(wxq.shape, const),
                  pl.BlockSpec(wxo.shape, const)],
        out_specs=pl.BlockSpec((tm, D_MODEL), row),
        compiler_params=_params(("arbitrary",)),
        name="mix_xattn",
    )(x2, hb, hb, hb, gate, o_f, o_b, kt, vm, convw, convg, gmat, glag, wout, xag, wxq, wxo)


def _mlp_kernel(x_ref, gain_ref, wu_ref, wd_ref, fgain_ref, o_ref):
    x = x_ref[...]
    r = _inv_rms(x)
    xg = (x * gain_ref[...]).astype(BF16)
    acc = None
    for c in range(D_FF // FF_CHUNK):
        sl = slice(c * FF_CHUNK, (c + 1) * FF_CHUNK)
        u = jnp.maximum(_dot(xg, wu_ref[:, sl]), 0.0)
        part = _dot((u * u).astype(BF16), wd_ref[sl, :])
        acc = part if acc is None else acc + part
    o_ref[...] = _rms(x + acc * (r * r), fgain_ref[...])


def _mlp(x2, gain, wu, wd, fgain):
    n = x2.shape[0]
    const = lambda i: (0, 0)
    row = lambda i: (i, 0)
    resident = lambda shape: pl.BlockSpec(shape, const, pipeline_mode=pl.Buffered(1))
    return pl.pallas_call(
        _mlp_kernel,
        out_shape=jax.ShapeDtypeStruct((n, D_MODEL), F32),
        grid=(n // MLP_TILE,),
        in_specs=[pl.BlockSpec((MLP_TILE, D_MODEL), row),
                  pl.BlockSpec(gain.shape, const),
                  resident(wu.shape),
                  resident(wd.shape),
                  pl.BlockSpec(fgain.shape, const)],
        out_specs=pl.BlockSpec((MLP_TILE, D_MODEL), row),
        compiler_params=_params(("arbitrary",)),
        name="mlp",
    )(x2, gain, wu, wd, fgain)


def _block_diag_ones(n, block):
    i = np.arange(n)
    return (i[:, None] // block) == (i[None, :] // block)


def _constant(values):
    return jnp.asarray(np.asarray(values, np.float32).astype(BF16))


def kernel(x, mem, mix_norm, w_in, conv_w, conv_norm, w_af, b_af, w_ab, b_ab, gla_norm, w_out, xa_norm, mem_norm, w_xq, w_xkv, w_xo, mlp_norm, w_up, w_down, final_norm):
    batch, seq, d = x.shape
    assert d == D_MODEL and seq % GLA_BLOCK == 0 and seq % TOKEN_TILE == 0
    assert seq % MIX_TILE == 0 and (batch * seq) % MLP_TILE == 0
    assert mix_norm.shape[0] == 1, "single-layer block"
    n = batch * seq
    x2 = x.reshape(n, d)
    row2 = lambda v: v.reshape(1, -1).astype(F32)

    cw = CONV_WIDTH
    ci = np.arange(CUMSUM_BLOCK)
    same_chunk = _block_diag_ones(CUMSUM_BLOCK, GLA_CHUNK)
    tril = _constant(same_chunk & (ci[None, :] <= ci[:, None]))
    triu = _constant(same_chunk & (ci[None, :] >= ci[:, None]))
    gmat = _constant(_block_diag_ones(cw, CONV_GROUP) / CONV_GROUP)

    (hb, qkv, gate, la), (wxkv, wout, wxq, wxo) = _in_proj(
        x2, row2(mix_norm[0]), w_in[0].T, w_af[0], w_ab[0], row2(b_af[0]), row2(b_ab[0]),
        (w_xkv[0], w_out[0], w_xq[0], w_xo[0]))
    kt, vm = _mem_kv(mem.reshape(batch * N_MEM, d), row2(mem_norm[0]), wxkv)
    (o_f, o_b), (wup, wdown) = _gla(qkv, la, tril, triu, batch, seq, (w_up[0], w_down[0]))
    convw = jnp.transpose(conv_w, (1, 0, 2)).astype(F32)
    xa = _mix_xattn(x2, hb, gate, o_f, o_b, kt, vm, convw, row2(conv_norm[0]), gmat,
                    row2(gla_norm[0]), wout, row2(xa_norm[0]), wxq, wxo, seq)
    out = _mlp(xa, row2(mlp_norm[0]), wup, wdown, row2(final_norm))
    return out.reshape(batch, seq, d)
```

```python
import functools

import jax
import jax.numpy as jnp
import numpy as np
from jax import lax
from jax.experimental import pallas as pl
from jax.experimental.pallas import tpu as pltpu

F32 = jnp.float32
BF16 = jnp.bfloat16

EPS = 1e-6
LOG2_E = 1.4426950408889634
D_MODEL = 1024
CONV_WIDTH = 512
CONV_GROUP = 64
GLA_HEADS = 4
GLA_DK = 64
GLA_DV = 128
GLA_K_TOTAL = GLA_HEADS * GLA_DK
GLA_V_TOTAL = GLA_HEADS * GLA_DV
GLA_LOWRANK = 16
GLA_GATE_NORM = 16.0
GLA_CHUNK = 64
N_MEM = 256
XA_HEADS = 4
XA_HEAD_DIM = D_MODEL // XA_HEADS
D_FF = 4 * D_MODEL

LANES = 128
BF16_SUBLANES = 16
VMEM_LIMIT_BYTES = 56 * 1024 * 1024

TOKEN_TILE = 1024
MIX_TILE = 1024
MLP_TILE = 1024
XA_ROW_SPLIT = 4
WEIGHT_PREP_ROWS = 128
GLA_BLOCK = 1024
CUMSUM_BLOCK = 256
FF_CHUNK = 4096


def _dot(a, b):
    return jnp.dot(a, b, preferred_element_type=F32)


def _dot_nt(a, b):
    return lax.dot_general(a, b, (((1,), (1,)), ((), ())), preferred_element_type=F32)


def _dot_tn(a, b):
    return lax.dot_general(a, b, (((0,), (0,)), ((), ())), preferred_element_type=F32)


def _inv_rms(x):
    return lax.rsqrt(jnp.mean(x * x, axis=-1, keepdims=True) + EPS)


def _rms(x, gain):
    return x * _inv_rms(x) * gain


def _params(semantics):
    return pltpu.CompilerParams(dimension_semantics=semantics,
                                vmem_limit_bytes=VMEM_LIMIT_BYTES)


def _cast_row_blocks(weights, steps):
    shapes = []
    for w in weights:
        assert w.dtype == F32 and w.shape[0] % (steps * BF16_SUBLANES) == 0
        shapes.append((w.shape[0] // steps, w.shape[1]))
    return shapes


def _cast_weights(src_refs, dst_refs, piece=0, n_pieces=1):
    for src_ref, dst_ref in zip(src_refs, dst_refs):
        slabs = min(n_pieces, src_ref.shape[1] // LANES)
        assert n_pieces % slabs == 0 and src_ref.shape[1] % (slabs * LANES) == 0
        every = n_pieces // slabs
        if piece % every == 0:
            cols = src_ref.shape[1] // slabs
            c0 = (piece // every) * cols
            dst_ref[:, c0:c0 + cols] = src_ref[:, c0:c0 + cols].astype(BF16)


def _in_proj_kernel(n_cast, x_ref, gain_ref, wt_ref, waf_ref, wab_ref, baf_ref, bab_ref, *refs):
    cast_src = refs[:n_cast]
    hb_ref, qkv_ref, gate_ref, la_ref = refs[n_cast:n_cast + 4]
    cast_dst = refs[n_cast + 4:2 * n_cast + 4]
    wc_ref, wg_ref, wa_ref = refs[2 * n_cast + 4:]
    cw = CONV_WIDTH
    o_q = 3 * cw
    o_k = o_q + GLA_K_TOTAL
    o_lr = o_q + 2 * GLA_K_TOTAL + 2 * GLA_V_TOTAL

    @pl.when(pl.program_id(0) == 0)
    def _():
        def convert(dst_ref, dst0, src0, rows, scale=None):
            for r in range(0, rows, WEIGHT_PREP_ROWS):
                nr = min(WEIGHT_PREP_ROWS, rows - r)
                blk = wt_ref[src0 + r:src0 + r + nr, :]
                if scale is not None:
                    blk = blk * scale
                dst_ref[dst0 + r:dst0 + r + nr, :] = blk.astype(BF16)

        convert(wc_ref, 0, cw, 2 * cw)
        convert(wc_ref, 2 * cw, 0, cw)
        wg_ref[0:LANES, :] = jnp.zeros((LANES, D_MODEL), BF16)
        convert(wg_ref, 0, o_lr, 2 * GLA_LOWRANK)
        convert(wg_ref, LANES, o_q, GLA_K_TOTAL, GLA_DK ** -0.5)
        convert(wg_ref, LANES + GLA_K_TOTAL, o_k, o_lr - o_k)
        wa_ref[...] = jnp.zeros(wa_ref.shape, BF16)
        wa_ref[0:GLA_LOWRANK, 0:GLA_K_TOTAL] = waf_ref[...].astype(BF16)
        wa_ref[GLA_LOWRANK:2 * GLA_LOWRANK, GLA_K_TOTAL:] = wab_ref[...].astype(BF16)

    x = x_ref[...]
    r = _inv_rms(x)
    xg = (x * gain_ref[...]).astype(BF16)
    qkv_cols = 2 * GLA_K_TOTAL + GLA_V_TOTAL
    zc = _dot_nt(xg, wc_ref[...])
    hb_ref[:, 0:cw] = (zc[:, 0:cw] * zc[:, cw:2 * cw] * (r * r)).astype(BF16)
    hb_ref[:, cw:2 * cw] = (zc[:, 2 * cw:3 * cw] * r).astype(BF16)
    _cast_weights(cast_src, cast_dst)
    zg = _dot_nt(xg, wg_ref[...]) * r
    qkv_ref[...] = zg[:, LANES:LANES + qkv_cols].astype(BF16)
    g = zg[:, LANES + qkv_cols:]
    gate_ref[...] = (g / (1.0 + jnp.exp(-g))).astype(BF16)
    codes = zg[:, 0:LANES].astype(BF16)
    z = _dot(codes, wa_ref[...]) + jnp.concatenate([baf_ref[...], bab_ref[...]], axis=1)
    l2a = ((jnp.minimum(z, 0.0) - jnp.log(1.0 + jnp.exp(-jnp.abs(z))))
           * (LOG2_E / GLA_GATE_NORM))
    la_ref[...] = l2a.astype(BF16)


def _in_proj(x2, gain, w_in_t, w_af, w_ab, b_af, b_ab, later_weights):
    n = x2.shape[0]
    steps = n // TOKEN_TILE
    cast_blocks = _cast_row_blocks(later_weights, steps)
    assert w_af.shape == w_ab.shape == (GLA_LOWRANK, GLA_K_TOTAL)
    assert b_af.shape == b_ab.shape == (1, GLA_K_TOTAL)
    const = lambda i: (0, 0)
    row = lambda i: (i, 0)
    conv_cols = 3 * CONV_WIDTH
    gla_cols = 2 * GLA_K_TOTAL + 2 * GLA_V_TOTAL
    assert w_in_t.shape == (conv_cols + gla_cols + 2 * GLA_LOWRANK, D_MODEL)
    outs = (2 * CONV_WIDTH, 2 * GLA_K_TOTAL + GLA_V_TOTAL, GLA_V_TOTAL, 2 * GLA_K_TOTAL)
    results = pl.pallas_call(
        functools.partial(_in_proj_kernel, len(later_weights)),
        out_shape=(tuple(jax.ShapeDtypeStruct((n, c), BF16) for c in outs)
                   + tuple(jax.ShapeDtypeStruct(w.shape, BF16) for w in later_weights)),
        grid=(steps,),
        in_specs=[pl.BlockSpec((TOKEN_TILE, D_MODEL), row),
                  pl.BlockSpec(gain.shape, const),
                  pl.BlockSpec(w_in_t.shape, const, pipeline_mode=pl.Buffered(1)),
                  pl.BlockSpec(w_af.shape, const),
                  pl.BlockSpec(w_ab.shape, const),
                  pl.BlockSpec(b_af.shape, const),
                  pl.BlockSpec(b_ab.shape, const)] + [pl.BlockSpec(b, row) for b in cast_blocks],
        out_specs=(tuple(pl.BlockSpec((TOKEN_TILE, c), row) for c in outs)
                   + tuple(pl.BlockSpec(b, row) for b in cast_blocks)),
        scratch_shapes=[pltpu.VMEM((conv_cols, D_MODEL), BF16),
                        pltpu.VMEM((LANES + gla_cols, D_MODEL), BF16),
                        pltpu.VMEM((LANES, 2 * GLA_K_TOTAL), BF16)],
        compiler_params=_params(("arbitrary",)),
        name="in_proj",
    )(x2, gain, w_in_t, w_af, w_ab, b_af, b_ab, *later_weights)
    return results[:len(outs)], results[len(outs):]


def _mem_kv(m_ref, gain_ref, w_ref, kt_ref, v_ref):
    h = _rms(m_ref[...], gain_ref[...]).astype(BF16)
    kv = _dot(h, w_ref[...])
    kt_ref[...] = kv[:, 0:D_MODEL].T.astype(BF16)
    v_ref[...] = kv[:, D_MODEL:].astype(BF16)


def _gla_prepare(qkv_ref, la_ref, tri_ref, qt_s, kt_s, kh_s, dec_s, reverse):
    tri = tri_ref[...]
    for r in range(GLA_BLOCK // CUMSUM_BLOCK):
        b = _dot(tri, la_ref[r * CUMSUM_BLOCK:(r + 1) * CUMSUM_BLOCK, :])
        grow = jnp.exp2(b)
        shrink = jnp.exp2(-b)
        rows = slice(r * CUMSUM_BLOCK, (r + 1) * CUMSUM_BLOCK)
        qt_s[rows, :] = qkv_ref[rows, 0:GLA_K_TOTAL] * grow.astype(BF16)
        kt = qkv_ref[rows, GLA_K_TOTAL:2 * GLA_K_TOTAL] * shrink.astype(BF16)
        kt_s[rows, :] = kt
        for cc in range(CUMSUM_BLOCK // GLA_CHUNK):
            c = r * (CUMSUM_BLOCK // GLA_CHUNK) + cc
            last = cc * GLA_CHUNK + (0 if reverse else GLA_CHUNK - 1)
            dec = grow[last:last + 1, :]
            dec_s[c] = dec
            crow = slice(cc * GLA_CHUNK, (cc + 1) * GLA_CHUNK)
            kh_s[c * GLA_CHUNK:(c + 1) * GLA_CHUNK, :] = kt[crow] * dec.astype(BF16)


def _gla_chunk(c, qkv_ref, o_ref, qt_s, kt_s, kh_s, dec_s, st_ref, reverse):
    rows = slice(c * GLA_CHUNK, (c + 1) * GLA_CHUNK)
    lane = lax.broadcasted_iota(jnp.int32, (GLA_CHUNK, LANES), 1)
    row = lax.broadcasted_iota(jnp.int32, (GLA_CHUNK, LANES), 0)
    first_head = lane < GLA_DK
    src = lane & (GLA_DK - 1)
    keep = (src > row) if reverse else (src <= row)
    vlane = lax.broadcasted_iota(jnp.int32, (GLA_CHUNK, 2 * GLA_DV), 1)
    first_head_v = vlane < GLA_DV
    dec = dec_s[c]
    zero = jnp.zeros((), BF16)
    for p in range(GLA_HEADS // 2):
        ksl = slice(p * LANES, (p + 1) * LANES)
        osl = slice(p * 2 * GLA_DV, (p + 1) * 2 * GLA_DV)
        vsl = slice(2 * GLA_K_TOTAL + p * 2 * GLA_DV, 2 * GLA_K_TOTAL + (p + 1) * 2 * GLA_DV)
        qc = qt_s[rows, ksl]
        kc = kt_s[rows, ksl]
        khc = kh_s[rows, ksl]
        vc = qkv_ref[rows, vsl]
        kbd = jnp.concatenate([jnp.where(first_head, kc, zero),
                               jnp.where(first_head, zero, kc)], axis=0)
        khbd = jnp.concatenate([jnp.where(first_head, khc, zero),
                                jnp.where(first_head, zero, khc)], axis=0)
        vbd = jnp.concatenate([jnp.where(first_head_v, vc, zero),
                               jnp.where(first_head_v, zero, vc)], axis=0)
        scores = _dot_nt(qc, kbd)
        scores = jnp.where(keep, scores, 0.0).astype(BF16)
        st = st_ref[p]
        o = _dot(scores, vbd) + _dot_nt(qc, st.astype(BF16))
        st_ref[p] = st * dec[:, ksl] + _dot_tn(vbd, khbd)
        o_ref[rows, osl] = o.astype(o_ref.dtype)


def _gla_kernel(n_cast, qkv_f_ref, la_f_ref, qkv_b_ref, la_b_ref, tril_ref, triu_ref, *refs):
    cast_src = refs[:n_cast]
    of_ref, ob_ref = refs[n_cast:n_cast + 2]
    cast_dst = refs[n_cast + 2:2 * n_cast + 2]
    qt_f, kt_f, kh_f, dec_f, st_f, qt_b, kt_b, kh_b, dec_b, st_b = refs[2 * n_cast + 2:]
    nchunk = GLA_BLOCK // GLA_CHUNK

    @pl.when(pl.program_id(1) == 0)
    def _():
        st_f[...] = jnp.zeros_like(st_f)
        st_b[...] = jnp.zeros_like(st_b)

    _gla_prepare(qkv_f_ref, la_f_ref, tril_ref, qt_f, kt_f, kh_f, dec_f, False)
    _gla_prepare(qkv_b_ref, la_b_ref, triu_ref, qt_b, kt_b, kh_b, dec_b, True)

    for c in range(nchunk):
        _gla_chunk(c, qkv_f_ref, of_ref, qt_f, kt_f, kh_f, dec_f, st_f, False)
        _gla_chunk(nchunk - 1 - c, qkv_b_ref, ob_ref, qt_b, kt_b, kh_b, dec_b, st_b, True)
        _cast_weights(cast_src, cast_dst, c, nchunk)


def _gla(qkv, la, tril, triu, batch, seq, later_weights):
    n = qkv.shape[0]
    nb = seq // GLA_BLOCK
    cast_blocks = _cast_row_blocks(later_weights, batch * nb)
    step_row = lambda b, i: (b * nb + i, 0)
    nchunk = GLA_BLOCK // GLA_CHUNK
    qkv_cols = 2 * GLA_K_TOTAL + GLA_V_TOTAL
    fwd = lambda b, i: (b * nb + i, 0)
    bwd = lambda b, i: (b * nb + nb - 1 - i, 0)
    bwd_la = lambda b, i: (b * nb + nb - 1 - i, 1)
    const = lambda b, i: (0, 0)
    dir_scratch = [pltpu.VMEM((GLA_BLOCK, GLA_K_TOTAL), BF16),
                   pltpu.VMEM((GLA_BLOCK, GLA_K_TOTAL), BF16),
                   pltpu.VMEM((GLA_BLOCK, GLA_K_TOTAL), BF16),
                   pltpu.VMEM((nchunk, 1, GLA_K_TOTAL), F32),
                   pltpu.VMEM((GLA_HEADS // 2, 2 * GLA_DV, LANES), F32)]
    results = pl.pallas_call(
        functools.partial(_gla_kernel, len(later_weights)),
        out_shape=((jax.ShapeDtypeStruct((n, GLA_V_TOTAL), BF16),
                    jax.ShapeDtypeStruct((n, GLA_V_TOTAL), BF16))
                   + tuple(jax.ShapeDtypeStruct(w.shape, BF16) for w in later_weights)),
        grid=(batch, nb),
        in_specs=[pl.BlockSpec((GLA_BLOCK, qkv_cols), fwd),
                  pl.BlockSpec((GLA_BLOCK, GLA_K_TOTAL), fwd),
                  pl.BlockSpec((GLA_BLOCK, qkv_cols), bwd),
                  pl.BlockSpec((GLA_BLOCK, GLA_K_TOTAL), bwd_la),
                  pl.BlockSpec(tril.shape, const),
                  pl.BlockSpec(triu.shape, const)] + [pl.BlockSpec(b, step_row) for b in cast_blocks],
        out_specs=((pl.BlockSpec((GLA_BLOCK, GLA_V_TOTAL), fwd),
                    pl.BlockSpec((GLA_BLOCK, GLA_V_TOTAL), bwd))
                   + tuple(pl.BlockSpec(b, step_row) for b in cast_blocks)),
        scratch_shapes=dir_scratch + dir_scratch,
        compiler_params=_params(("arbitrary", "arbitrary")),
        name="gla",
    )(qkv, la, qkv, la, tril, triu, *later_weights)
    return results[:2], results[2:]


def _mix_xattn_kernel(tiles_per_seq, x_ref, hb_ref, prev_ref, next_ref, gate_ref, of_ref, ob_ref,
                      mem_ref, memg_ref, wxkv_ref, convw_ref, convg_ref, gmat_ref, glag_ref,
                      wout_ref, xag_ref, wxq_ref, wxo_ref, o_ref, kt_ref, vm_ref):
    tm = x_ref.shape[0]
    t = pl.program_id(0) % tiles_per_seq
    has_prev = (t > 0).astype(F32)
    has_next = (t < tiles_per_seq - 1).astype(F32)

    @pl.when(t == 0)
    def _():
        _mem_kv(mem_ref, memg_ref, wxkv_ref, kt_ref, vm_ref)

    h = hb_ref[:, 0:CONV_WIDTH].astype(F32)
    last = BF16_SUBLANES - 1
    h_before = prev_ref[last:last + 1, :].astype(F32) * has_prev
    h_after = next_ref[0:1, :].astype(F32) * has_next
    rowid = lax.broadcasted_iota(jnp.int32, (tm, CONV_WIDTH), 0)
    h_m1 = jnp.where(rowid == 0, h_before, pltpu.roll(h, 1, 0))
    h_p1 = jnp.where(rowid == tm - 1, h_after, pltpu.roll(h, tm - 1, 0))
    y = convw_ref[0] * h_m1 + convw_ref[1] * h + convw_ref[2] * h_p1
    y = hb_ref[:, CONV_WIDTH:2 * CONV_WIDTH].astype(F32) * y
    ms = _dot((y * y).astype(BF16), gmat_ref[...])
    y = y * lax.rsqrt(ms + EPS) * convg_ref[...]

    o = of_ref[...].astype(F32) + ob_ref[...].astype(F32)
    heads = []
    for hd in range(GLA_HEADS):
        oh = o[:, hd * GLA_DV:(hd + 1) * GLA_DV]
        heads.append(_rms(oh, glag_ref[...]))
    o = jnp.concatenate(heads, axis=1) * gate_ref[...].astype(F32)

    mixed = jnp.concatenate([y.astype(BF16), o.astype(BF16)], axis=1)
    x1 = x_ref[...] + _dot(mixed, wout_ref[...])

    r = _inv_rms(x1) * (XA_HEAD_DIM ** -0.5 * LOG2_E)
    q = (_dot((x1 * xag_ref[...]).astype(BF16), wxq_ref[...]) * r).astype(BF16)
    half_rows = tm // XA_ROW_SPLIT
    halves = []
    for top in range(0, tm, half_rows):
        rows = slice(top, top + half_rows)
        outs = []
        for hd in range(XA_HEADS):
            sl = slice(hd * XA_HEAD_DIM, (hd + 1) * XA_HEAD_DIM)
            s = _dot(q[rows, sl], kt_ref[sl, :])
            s = s - jnp.max(s, axis=-1, keepdims=True)
            e = jnp.exp2(s)
            p = e / jnp.sum(e, axis=-1, keepdims=True)
            outs.append(_dot(p.astype(BF16), vm_ref[:, sl]).astype(BF16))
        halves.append(jnp.concatenate(outs, axis=1))
    att = jnp.concatenate(halves, axis=0)
    o_ref[...] = x1 + _dot(att, wxo_ref[...])


def _mix_xattn(x2, hb, gate, o_f, o_b, mem2, memg, wxkv, convw, convg, gmat, glag, wout, xag, wxq,
               wxo, seq):
    n = x2.shape[0]
    tm = MIX_TILE
    tiles_per_seq = seq // tm
    halo_per_tile = tm // BF16_SUBLANES
    n_halo = n // BF16_SUBLANES
    const = lambda i: (0, 0)
    row = lambda i: (i, 0)
    return pl.pallas_call(
        functools.partial(_mix_xattn_kernel, tiles_per_seq),
        out_shape=jax.ShapeDtypeStruct((n, D_MODEL), F32),
        grid=(n // tm,),
        in_specs=[pl.BlockSpec((tm, D_MODEL), row),
                  pl.BlockSpec((tm, 2 * CONV_WIDTH), row),
                  pl.BlockSpec((BF16_SUBLANES, CONV_WIDTH),
                               lambda i: (jnp.maximum(i * halo_per_tile - 1, 0), 0)),
                  pl.BlockSpec((BF16_SUBLANES, CONV_WIDTH),
                               lambda i: (jnp.minimum((i + 1) * halo_per_tile, n_halo - 1), 0)),
                  pl.BlockSpec((tm, GLA_V_TOTAL), row),
                  pl.BlockSpec((tm, GLA_V_TOTAL), row),
                  pl.BlockSpec((tm, GLA_V_TOTAL), row),
                  pl.BlockSpec((N_MEM, D_MODEL), lambda i: (i // tiles_per_seq, 0)),
                  pl.BlockSpec(memg.shape, const),
                  pl.BlockSpec(wxkv.shape, const, pipeline_mode=pl.Buffered(1)),
                  pl.BlockSpec(convw.shape, lambda i: (0, 0, 0)),
                  pl.BlockSpec(convg.shape, const),
                  pl.BlockSpec(gmat.shape, const),
                  pl.BlockSpec(glag.shape, const),
                  pl.BlockSpec(wout.shape, const),
                  pl.BlockSpec(xag.shape, const),
                  pl.BlockSpec(wxq.shape, const),
                  pl.BlockSpec(wxo.shape, const)],
        out_specs=pl.BlockSpec((tm, D_MODEL), row),
        scratch_shapes=[pltpu.VMEM((D_MODEL, N_MEM), BF16),
                        pltpu.VMEM((N_MEM, D_MODEL), BF16)],
        compiler_params=_params(("arbitrary",)),
        name="mix_xattn",
    )(x2, hb, hb, hb, gate, o_f, o_b, mem2, memg, wxkv, convw, convg, gmat, glag, wout, xag, wxq,
      wxo)


def _mlp_kernel(x_ref, gain_ref, wu_ref, wd_ref, fgain_ref, o_ref):
    x = x_ref[...]
    r = _inv_rms(x)
    xg = (x * gain_ref[...]).astype(BF16)
    acc = None
    for c in range(D_FF // FF_CHUNK):
        sl = slice(c * FF_CHUNK, (c + 1) * FF_CHUNK)
        u = jnp.maximum(_dot(xg, wu_ref[:, sl]), 0.0)
        part = _dot((u * u).astype(BF16), wd_ref[sl, :])
        acc = part if acc is None else acc + part
    o_ref[...] = _rms(x + acc * (r * r), fgain_ref[...])


def _mlp(x2, gain, wu, wd, fgain):
    n = x2.shape[0]
    const = lambda i: (0, 0)
    row = lambda i: (i, 0)
    resident = lambda shape: pl.BlockSpec(shape, const, pipeline_mode=pl.Buffered(1))
    return pl.pallas_call(
        _mlp_kernel,
        out_shape=jax.ShapeDtypeStruct((n, D_MODEL), F32),
        grid=(n // MLP_TILE,),
        in_specs=[pl.BlockSpec((MLP_TILE, D_MODEL), row),
                  pl.BlockSpec(gain.shape, const),
                  resident(wu.shape),
                  resident(wd.shape),
                  pl.BlockSpec(fgain.shape, const)],
        out_specs=pl.BlockSpec((MLP_TILE, D_MODEL), row),
        compiler_params=_params(("arbitrary",)),
        name="mlp",
    )(x2, gain, wu, wd, fgain)


def _block_diag_ones(n, block):
    i = np.arange(n)
    return (i[:, None] // block) == (i[None, :] // block)


def _constant(values):
    return jnp.asarray(np.asarray(values, np.float32).astype(BF16))


def kernel(x, mem, mix_norm, w_in, conv_w, conv_norm, w_af, b_af, w_ab, b_ab, gla_norm, w_out, xa_norm, mem_norm, w_xq, w_xkv, w_xo, mlp_norm, w_up, w_down, final_norm):
    batch, seq, d = x.shape
    assert d == D_MODEL and seq % GLA_BLOCK == 0 and seq % TOKEN_TILE == 0
    assert seq % MIX_TILE == 0 and (batch * seq) % MLP_TILE == 0
    assert mix_norm.shape[0] == 1, "single-layer block"
    n = batch * seq
    x2 = x.reshape(n, d)
    row2 = lambda v: v.reshape(1, -1).astype(F32)

    cw = CONV_WIDTH
    ci = np.arange(CUMSUM_BLOCK)
    same_chunk = _block_diag_ones(CUMSUM_BLOCK, GLA_CHUNK)
    tril = _constant(same_chunk & (ci[None, :] <= ci[:, None]))
    triu = _constant(same_chunk & (ci[None, :] >= ci[:, None]))
    gmat = _constant(_block_diag_ones(cw, CONV_GROUP) / CONV_GROUP)

    (hb, qkv, gate, la), (wxkv, wout, wxq, wxo) = _in_proj(
        x2, row2(mix_norm[0]), w_in[0].T, w_af[0], w_ab[0], row2(b_af[0]), row2(b_ab[0]),
        (w_xkv[0], w_out[0], w_xq[0], w_xo[0]))
    (o_f, o_b), (wup, wdown) = _gla(qkv, la, tril, triu, batch, seq, (w_up[0], w_down[0]))
    convw = jnp.transpose(conv_w, (1, 0, 2)).astype(F32)
    xa = _mix_xattn(x2, hb, gate, o_f, o_b, mem.reshape(batch * N_MEM, d), row2(mem_norm[0]), wxkv,
                    convw, row2(conv_norm[0]), gmat,
                    row2(gla_norm[0]), wout, row2(xa_norm[0]), wxq, wxo, seq)
    out = _mlp(xa, row2(mlp_norm[0]), wup, wdown, row2(final_norm))
    return out.reshape(batch, seq, d)
```

```python
import functools

import jax
import jax.numpy as jnp
import numpy as np
from jax import lax
from jax.experimental import pallas as pl
from jax.experimental.pallas import tpu as pltpu

F32 = jnp.float32
BF16 = jnp.bfloat16

EPS = 1e-6
LOG2_E = 1.4426950408889634
D_MODEL = 1024
CONV_WIDTH = 512
CONV_GROUP = 64
GLA_HEADS = 4
GLA_DK = 64
GLA_DV = 128
GLA_K_TOTAL = GLA_HEADS * GLA_DK
GLA_V_TOTAL = GLA_HEADS * GLA_DV
GLA_LOWRANK = 16
GLA_GATE_NORM = 16.0
GLA_CHUNK = 64
N_MEM = 256
XA_HEADS = 4
XA_HEAD_DIM = D_MODEL // XA_HEADS
D_FF = 4 * D_MODEL

LANES = 128
BF16_SUBLANES = 16
VMEM_LIMIT_BYTES = 56 * 1024 * 1024

TOKEN_TILE = 1024
MIX_TILE = 1024
MLP_TILE = 1024
XA_ROW_SPLIT = 4
WEIGHT_PREP_ROWS = 128
GLA_BLOCK = 1024
CUMSUM_BLOCK = 256
FF_CHUNK = 4096


def _dot(a, b):
    return jnp.dot(a, b, preferred_element_type=F32)


def _dot_nt(a, b):
    return lax.dot_general(a, b, (((1,), (1,)), ((), ())), preferred_element_type=F32)


def _dot_tn(a, b):
    return lax.dot_general(a, b, (((0,), (0,)), ((), ())), preferred_element_type=F32)


def _inv_rms(x):
    return lax.rsqrt(jnp.mean(x * x, axis=-1, keepdims=True) + EPS)


def _rms(x, gain):
    return x * _inv_rms(x) * gain


def _params(semantics):
    return pltpu.CompilerParams(dimension_semantics=semantics,
                                vmem_limit_bytes=VMEM_LIMIT_BYTES)


def _cast_row_blocks(weights, steps):
    shapes = []
    for w in weights:
        assert w.dtype == F32 and w.shape[0] % (steps * BF16_SUBLANES) == 0
        shapes.append((w.shape[0] // steps, w.shape[1]))
    return shapes


def _cast_weights(src_refs, dst_refs, piece=0, n_pieces=1):
    for src_ref, dst_ref in zip(src_refs, dst_refs):
        slabs = min(n_pieces, src_ref.shape[1] // LANES)
        assert n_pieces % slabs == 0 and src_ref.shape[1] % (slabs * LANES) == 0
        every = n_pieces // slabs
        if piece % every == 0:
            cols = src_ref.shape[1] // slabs
            c0 = (piece // every) * cols
            dst_ref[:, c0:c0 + cols] = src_ref[:, c0:c0 + cols].astype(BF16)


def _in_proj_kernel(n_cast, x_ref, gain_ref, wt_ref, waf_ref, wab_ref, baf_ref, bab_ref, *refs):
    cast_src = refs[:n_cast]
    hb_ref, qkv_ref, gate_ref, la_ref = refs[n_cast:n_cast + 4]
    cast_dst = refs[n_cast + 4:2 * n_cast + 4]
    wc_ref, wg_ref, wa_ref = refs[2 * n_cast + 4:]
    cw = CONV_WIDTH
    o_q = 3 * cw
    o_k = o_q + GLA_K_TOTAL
    o_lr = o_q + 2 * GLA_K_TOTAL + 2 * GLA_V_TOTAL

    @pl.when(pl.program_id(0) == 0)
    def _():
        def convert(dst_ref, dst0, src0, rows, scale=None):
            for r in range(0, rows, WEIGHT_PREP_ROWS):
                nr = min(WEIGHT_PREP_ROWS, rows - r)
                blk = wt_ref[src0 + r:src0 + r + nr, :]
                if scale is not None:
                    blk = blk * scale
                dst_ref[dst0 + r:dst0 + r + nr, :] = blk.astype(BF16)

        convert(wc_ref, 0, cw, 2 * cw)
        convert(wc_ref, 2 * cw, 0, cw)
        wg_ref[0:LANES, :] = jnp.zeros((LANES, D_MODEL), BF16)
        convert(wg_ref, 0, o_lr, 2 * GLA_LOWRANK)
        convert(wg_ref, LANES, o_q, GLA_K_TOTAL, GLA_DK ** -0.5)
        convert(wg_ref, LANES + GLA_K_TOTAL, o_k, o_lr - o_k)
        wa_ref[...] = jnp.zeros(wa_ref.shape, BF16)
        wa_ref[0:GLA_LOWRANK, 0:GLA_K_TOTAL] = waf_ref[...].astype(BF16)
        wa_ref[GLA_LOWRANK:2 * GLA_LOWRANK, GLA_K_TOTAL:] = wab_ref[...].astype(BF16)

    x = x_ref[...]
    r = _inv_rms(x)
    xg = (x * gain_ref[...]).astype(BF16)
    qkv_cols = 2 * GLA_K_TOTAL + GLA_V_TOTAL
    zc = _dot_nt(xg, wc_ref[...])
    hb_ref[:, 0:cw] = (zc[:, 0:cw] * zc[:, cw:2 * cw] * (r * r)).astype(BF16)
    hb_ref[:, cw:2 * cw] = (zc[:, 2 * cw:3 * cw] * r).astype(BF16)
    _cast_weights(cast_src, cast_dst)
    zg = _dot_nt(xg, wg_ref[...]) * r
    qkv_ref[...] = zg[:, LANES:LANES + qkv_cols].astype(BF16)
    g = zg[:, LANES + qkv_cols:]
    gate_ref[...] = (g / (1.0 + jnp.exp(-g))).astype(BF16)
    codes = zg[:, 0:LANES].astype(BF16)
    z = _dot(codes, wa_ref[...]) + jnp.concatenate([baf_ref[...], bab_ref[...]], axis=1)
    l2a = ((jnp.minimum(z, 0.0) - jnp.log(1.0 + jnp.exp(-jnp.abs(z))))
           * (LOG2_E / GLA_GATE_NORM))
    la_ref[...] = l2a.astype(BF16)


def _in_proj(x2, gain, w_in_t, w_af, w_ab, b_af, b_ab, later_weights):
    n = x2.shape[0]
    steps = n // TOKEN_TILE
    cast_blocks = _cast_row_blocks(later_weights, steps)
    assert w_af.shape == w_ab.shape == (GLA_LOWRANK, GLA_K_TOTAL)
    assert b_af.shape == b_ab.shape == (1, GLA_K_TOTAL)
    const = lambda i: (0, 0)
    row = lambda i: (i, 0)
    conv_cols = 3 * CONV_WIDTH
    gla_cols = 2 * GLA_K_TOTAL + 2 * GLA_V_TOTAL
    assert w_in_t.shape == (conv_cols + gla_cols + 2 * GLA_LOWRANK, D_MODEL)
    outs = (2 * CONV_WIDTH, 2 * GLA_K_TOTAL + GLA_V_TOTAL, GLA_V_TOTAL, 2 * GLA_K_TOTAL)
    results = pl.pallas_call(
        functools.partial(_in_proj_kernel, len(later_weights)),
        out_shape=(tuple(jax.ShapeDtypeStruct((n, c), BF16) for c in outs)
                   + tuple(jax.ShapeDtypeStruct(w.shape, BF16) for w in later_weights)),
        grid=(steps,),
        in_specs=[pl.BlockSpec((TOKEN_TILE, D_MODEL), row),
                  pl.BlockSpec(gain.shape, const),
                  pl.BlockSpec(w_in_t.shape, const, pipeline_mode=pl.Buffered(1)),
                  pl.BlockSpec(w_af.shape, const),
                  pl.BlockSpec(w_ab.shape, const),
                  pl.BlockSpec(b_af.shape, const),
                  pl.BlockSpec(b_ab.shape, const)] + [pl.BlockSpec(b, row) for b in cast_blocks],
        out_specs=(tuple(pl.BlockSpec((TOKEN_TILE, c), row) for c in outs)
                   + tuple(pl.BlockSpec(b, row) for b in cast_blocks)),
        scratch_shapes=[pltpu.VMEM((conv_cols, D_MODEL), BF16),
                        pltpu.VMEM((LANES + gla_cols, D_MODEL), BF16),
                        pltpu.VMEM((LANES, 2 * GLA_K_TOTAL), BF16)],
        compiler_params=_params(("arbitrary",)),
        name="in_proj",
    )(x2, gain, w_in_t, w_af, w_ab, b_af, b_ab, *later_weights)
    return results[:len(outs)], results[len(outs):]


def _mem_kv(m_ref, gain_ref, w_ref, kt_ref, v_ref):
    h = _rms(m_ref[...], gain_ref[...]).astype(BF16)
    kv = _dot(h, w_ref[...])
    kt_ref[...] = kv[:, 0:D_MODEL].T.astype(BF16)
    v_ref[...] = kv[:, D_MODEL:].astype(BF16)


def _gla_prepare(qkv_ref, la_ref, tri_ref, qt_s, kt_s, kh_s, dec_s, reverse):
    tri = tri_ref[...]
    for r in range(GLA_BLOCK // CUMSUM_BLOCK):
        b = _dot(tri, la_ref[r * CUMSUM_BLOCK:(r + 1) * CUMSUM_BLOCK, :])
        grow = jnp.exp2(b)
        shrink = jnp.exp2(-b)
        rows = slice(r * CUMSUM_BLOCK, (r + 1) * CUMSUM_BLOCK)
        qt_s[rows, :] = qkv_ref[rows, 0:GLA_K_TOTAL] * grow.astype(BF16)
        kt = qkv_ref[rows, GLA_K_TOTAL:2 * GLA_K_TOTAL] * shrink.astype(BF16)
        kt_s[rows, :] = kt
        for cc in range(CUMSUM_BLOCK // GLA_CHUNK):
            c = r * (CUMSUM_BLOCK // GLA_CHUNK) + cc
            last = cc * GLA_CHUNK + (0 if reverse else GLA_CHUNK - 1)
            dec = grow[last:last + 1, :]
            dec_s[c] = dec
            crow = slice(cc * GLA_CHUNK, (cc + 1) * GLA_CHUNK)
            kh_s[c * GLA_CHUNK:(c + 1) * GLA_CHUNK, :] = kt[crow] * dec.astype(BF16)


def _gla_chunk(c, qkv_ref, o_ref, qt_s, kt_s, kh_s, dec_s, st_ref, reverse):
    rows = slice(c * GLA_CHUNK, (c + 1) * GLA_CHUNK)
    lane = lax.broadcasted_iota(jnp.int32, (GLA_CHUNK, LANES), 1)
    row = lax.broadcasted_iota(jnp.int32, (GLA_CHUNK, LANES), 0)
    first_head = lane < GLA_DK
    src = lane & (GLA_DK - 1)
    keep = (src > row) if reverse else (src <= row)
    vlane = lax.broadcasted_iota(jnp.int32, (GLA_CHUNK, 2 * GLA_DV), 1)
    first_head_v = vlane < GLA_DV
    dec = dec_s[c]
    zero = jnp.zeros((), BF16)
    for p in range(GLA_HEADS // 2):
        ksl = slice(p * LANES, (p + 1) * LANES)
        osl = slice(p * 2 * GLA_DV, (p + 1) * 2 * GLA_DV)
        vsl = slice(2 * GLA_K_TOTAL + p * 2 * GLA_DV, 2 * GLA_K_TOTAL + (p + 1) * 2 * GLA_DV)
        qc = qt_s[rows, ksl]
        kc = kt_s[rows, ksl]
        khc = kh_s[rows, ksl]
        vc = qkv_ref[rows, vsl]
        kbd = jnp.concatenate([jnp.where(first_head, kc, zero),
                               jnp.where(first_head, zero, kc)], axis=0)
        khbd = jnp.concatenate([jnp.where(first_head, khc, zero),
                                jnp.where(first_head, zero, khc)], axis=0)
        vbd = jnp.concatenate([jnp.where(first_head_v, vc, zero),
                               jnp.where(first_head_v, zero, vc)], axis=0)
        scores = _dot_nt(qc, kbd)
        scores = jnp.where(keep, scores, 0.0).astype(BF16)
        st = st_ref[p]
        o = _dot(scores, vbd) + _dot_nt(qc, st.astype(BF16))
        st_ref[p] = st * dec[:, ksl] + _dot_tn(vbd, khbd)
        o_ref[rows, osl] = o.astype(o_ref.dtype)


def _gla_kernel(n_cast, qkv_f_ref, la_f_ref, qkv_b_ref, la_b_ref, tril_ref, triu_ref, *refs):
    cast_src = refs[:n_cast]
    of_ref, ob_ref = refs[n_cast:n_cast + 2]
    cast_dst = refs[n_cast + 2:2 * n_cast + 2]
    qt_f, kt_f, kh_f, dec_f, st_f, qt_b, kt_b, kh_b, dec_b, st_b = refs[2 * n_cast + 2:]
    nchunk = GLA_BLOCK // GLA_CHUNK

    @pl.when(pl.program_id(1) == 0)
    def _():
        st_f[...] = jnp.zeros_like(st_f)
        st_b[...] = jnp.zeros_like(st_b)

    _gla_prepare(qkv_f_ref, la_f_ref, tril_ref, qt_f, kt_f, kh_f, dec_f, False)
    _gla_prepare(qkv_b_ref, la_b_ref, triu_ref, qt_b, kt_b, kh_b, dec_b, True)

    for c in range(nchunk):
        _gla_chunk(c, qkv_f_ref, of_ref, qt_f, kt_f, kh_f, dec_f, st_f, False)
        _gla_chunk(nchunk - 1 - c, qkv_b_ref, ob_ref, qt_b, kt_b, kh_b, dec_b, st_b, True)
        _cast_weights(cast_src, cast_dst, c, nchunk)


def _gla(qkv, la, tril, triu, batch, seq, later_weights):
    n = qkv.shape[0]
    nb = seq // GLA_BLOCK
    cast_blocks = _cast_row_blocks(later_weights, batch * nb)
    step_row = lambda b, i: (b * nb + i, 0)
    nchunk = GLA_BLOCK // GLA_CHUNK
    qkv_cols = 2 * GLA_K_TOTAL + GLA_V_TOTAL
    fwd = lambda b, i: (b * nb + i, 0)
    bwd = lambda b, i: (b * nb + nb - 1 - i, 0)
    bwd_la = lambda b, i: (b * nb + nb - 1 - i, 1)
    const = lambda b, i: (0, 0)
    dir_scratch = [pltpu.VMEM((GLA_BLOCK, GLA_K_TOTAL), BF16),
                   pltpu.VMEM((GLA_BLOCK, GLA_K_TOTAL), BF16),
                   pltpu.VMEM((GLA_BLOCK, GLA_K_TOTAL), BF16),
                   pltpu.VMEM((nchunk, 1, GLA_K_TOTAL), F32),
                   pltpu.VMEM((GLA_HEADS // 2, 2 * GLA_DV, LANES), F32)]
    results = pl.pallas_call(
        functools.partial(_gla_kernel, len(later_weights)),
        out_shape=((jax.ShapeDtypeStruct((n, GLA_V_TOTAL), BF16),
                    jax.ShapeDtypeStruct((n, GLA_V_TOTAL), BF16))
                   + tuple(jax.ShapeDtypeStruct(w.shape, BF16) for w in later_weights)),
        grid=(batch, nb),
        in_specs=[pl.BlockSpec((GLA_BLOCK, qkv_cols), fwd),
                  pl.BlockSpec((GLA_BLOCK, GLA_K_TOTAL), fwd),
                  pl.BlockSpec((GLA_BLOCK, qkv_cols), bwd),
                  pl.BlockSpec((GLA_BLOCK, GLA_K_TOTAL), bwd_la),
                  pl.BlockSpec(tril.shape, const),
                  pl.BlockSpec(triu.shape, const)] + [pl.BlockSpec(b, step_row) for b in cast_blocks],
        out_specs=((pl.BlockSpec((GLA_BLOCK, GLA_V_TOTAL), fwd),
                    pl.BlockSpec((GLA_BLOCK, GLA_V_TOTAL), bwd))
                   + tuple(pl.BlockSpec(b, step_row) for b in cast_blocks)),
        scratch_shapes=dir_scratch + dir_scratch,
        compiler_params=_params(("arbitrary", "arbitrary")),
        name="gla",
    )(qkv, la, qkv, la, tril, triu, *later_weights)
    return results[:2], results[2:]


def _mix_xattn_kernel(tiles_per_seq, x_ref, hb_ref, prev_ref, next_ref, gate_ref, of_ref, ob_ref,
                      mem_ref, memg_ref, wxkv_ref, convw_ref, convg_ref, gmat_ref, glag_ref,
                      wout_ref, xag_ref, wxq_ref, wxo_ref, o_ref, kt_ref, vm_ref):
    tm = x_ref.shape[0]
    t = pl.program_id(0) % tiles_per_seq
    has_prev = (t > 0).astype(F32)
    has_next = (t < tiles_per_seq - 1).astype(F32)

    @pl.when(t == 0)
    def _():
        _mem_kv(mem_ref, memg_ref, wxkv_ref, kt_ref, vm_ref)

    h = hb_ref[:, 0:CONV_WIDTH].astype(F32)
    last = BF16_SUBLANES - 1
    h_before = prev_ref[last:last + 1, :].astype(F32) * has_prev
    h_after = next_ref[0:1, :].astype(F32) * has_next
    rowid = lax.broadcasted_iota(jnp.int32, (tm, CONV_WIDTH), 0)
    h_m1 = jnp.where(rowid == 0, h_before, pltpu.roll(h, 1, 0))
    h_p1 = jnp.where(rowid == tm - 1, h_after, pltpu.roll(h, tm - 1, 0))
    y = convw_ref[0] * h_m1 + convw_ref[1] * h + convw_ref[2] * h_p1
    y = hb_ref[:, CONV_WIDTH:2 * CONV_WIDTH].astype(F32) * y
    ms = _dot((y * y).astype(BF16), gmat_ref[...])
    y = y * lax.rsqrt(ms + EPS) * convg_ref[...]

    o = of_ref[...].astype(F32) + ob_ref[...].astype(F32)
    heads = []
    for hd in range(GLA_HEADS):
        oh = o[:, hd * GLA_DV:(hd + 1) * GLA_DV]
        heads.append(_rms(oh, glag_ref[...]))
    o = jnp.concatenate(heads, axis=1) * gate_ref[...].astype(F32)

    mixed = jnp.concatenate([y.astype(BF16), o.astype(BF16)], axis=1)
    x1 = x_ref[...] + _dot(mixed, wout_ref[...])

    r = _inv_rms(x1) * (XA_HEAD_DIM ** -0.5 * LOG2_E)
    q = (_dot((x1 * xag_ref[...]).astype(BF16), wxq_ref[...]) * r).astype(BF16)
    half_rows = tm // XA_ROW_SPLIT
    halves = []
    for top in range(0, tm, half_rows):
        rows = slice(top, top + half_rows)
        outs = []
        for hd in range(XA_HEADS):
            sl = slice(hd * XA_HEAD_DIM, (hd + 1) * XA_HEAD_DIM)
            s = _dot(q[rows, sl], kt_ref[sl, :])
            s = s - jnp.max(s, axis=-1, keepdims=True)
            e = jnp.exp2(s)
            p = e / jnp.sum(e, axis=-1, keepdims=True)
            outs.append(_dot(p.astype(BF16), vm_ref[:, sl]).astype(BF16))
        halves.append(jnp.concatenate(outs, axis=1))
    att = jnp.concatenate(halves, axis=0)
    o_ref[...] = x1 + _dot(att, wxo_ref[...])


def _mix_xattn(x2, hb, gate, o_f, o_b, mem2, memg, wxkv, convw, convg, gmat, glag, wout, xag, wxq,
               wxo, seq):
    n = x2.shape[0]
    tm = MIX_TILE
    tiles_per_seq = seq // tm
    halo_per_tile = tm // BF16_SUBLANES
    n_halo = n // BF16_SUBLANES
    const = lambda i: (0, 0)
    row = lambda i: (i, 0)
    return pl.pallas_call(
        functools.partial(_mix_xattn_kernel, tiles_per_seq),
        out_shape=jax.ShapeDtypeStruct((n, D_MODEL), F32),
        grid=(n // tm,),
        in_specs=[pl.BlockSpec((tm, D_MODEL), row),
                  pl.BlockSpec((tm, 2 * CONV_WIDTH), row),
                  pl.BlockSpec((BF16_SUBLANES, CONV_WIDTH),
                               lambda i: (jnp.maximum(i * halo_per_tile - 1, 0), 0)),
                  pl.BlockSpec((BF16_SUBLANES, CONV_WIDTH),
                               lambda i: (jnp.minimum((i + 1) * halo_per_tile, n_halo - 1), 0)),
                  pl.BlockSpec((tm, GLA_V_TOTAL), row),
                  pl.BlockSpec((tm, GLA_V_TOTAL), row),
                  pl.BlockSpec((tm, GLA_V_TOTAL), row),
                  pl.BlockSpec((N_MEM, D_MODEL), lambda i: (i // tiles_per_seq, 0)),
                  pl.BlockSpec(memg.shape, const),
                  pl.BlockSpec(wxkv.shape, const, pipeline_mode=pl.Buffered(1)),
                  pl.BlockSpec(convw.shape, lambda i: (0, 0, 0)),
                  pl.BlockSpec(convg.shape, const),
                  pl.BlockSpec(gmat.shape, const),
                  pl.BlockSpec(glag.shape, const),
                  pl.BlockSpec(wout.shape, const),
                  pl.BlockSpec(xag.shape, const),
                  pl.BlockSpec(wxq.shape, const),
                  pl.BlockSpec(wxo.shape, const)],
        out_specs=pl.BlockSpec((tm, D_MODEL), row),
        scratch_shapes=[pltpu.VMEM((D_MODEL, N_MEM), BF16),
                        pltpu.VMEM((N_MEM, D_MODEL), BF16)],
        compiler_params=_params(("arbitrary",)),
        name="mix_xattn",
    )(x2, hb, hb, hb, gate, o_f, o_b, mem2, memg, wxkv, convw, convg, gmat, glag, wout, xag, wxq,
      wxo)


def _mlp_kernel(x_ref, gain_ref, wu_ref, wd_ref, fgain_ref, o_ref):
    x = x_ref[...]
    r = _inv_rms(x)
    xg = (x * gain_ref[...]).astype(BF16)
    acc = None
    for c in range(D_FF // FF_CHUNK):
        sl = slice(c * FF_CHUNK, (c + 1) * FF_CHUNK)
        u = jnp.maximum(_dot(xg, wu_ref[:, sl]), 0.0)
        part = _dot((u * u).astype(BF16), wd_ref[sl, :])
        acc = part if acc is None else acc + part
    o_ref[...] = _rms(x + acc * (r * r), fgain_ref[...])


def _mlp(x2, gain, wu, wd, fgain):
    n = x2.shape[0]
    const = lambda i: (0, 0)
    row = lambda i: (i, 0)
    resident = lambda shape: pl.BlockSpec(shape, const, pipeline_mode=pl.Buffered(1))
    return pl.pallas_call(
        _mlp_kernel,
        out_shape=jax.ShapeDtypeStruct((n, D_MODEL), F32),
        grid=(n // MLP_TILE,),
        in_specs=[pl.BlockSpec((MLP_TILE, D_MODEL), row),
                  pl.BlockSpec(gain.shape, const),
                  resident(wu.shape),
                  resident(wd.shape),
                  pl.BlockSpec(fgain.shape, const)],
        out_specs=pl.BlockSpec((MLP_TILE, D_MODEL), row),
        compiler_params=_params(("arbitrary",)),
        name="mlp",
    )(x2, gain, wu, wd, fgain)


def _block_diag_ones(n, block):
    i = np.arange(n)
    return (i[:, None] // block) == (i[None, :] // block)


def _constant(values):
    return jnp.asarray(np.asarray(values, np.float32).astype(BF16))


def kernel(x, mem, mix_norm, w_in, conv_w, conv_norm, w_af, b_af, w_ab, b_ab, gla_norm, w_out, xa_norm, mem_norm, w_xq, w_xkv, w_xo, mlp_norm, w_up, w_down, final_norm):
    batch, seq, d = x.shape
    assert d == D_MODEL and seq % GLA_BLOCK == 0 and seq % TOKEN_TILE == 0
    assert seq % MIX_TILE == 0 and (batch * seq) % MLP_TILE == 0
    assert mix_norm.shape[0] == 1, "single-layer block"
    n = batch * seq
    x2 = x.reshape(n, d)
    row2 = lambda v: v.reshape(1, -1).astype(F32)

    cw = CONV_WIDTH
    ci = np.arange(CUMSUM_BLOCK)
    same_chunk = _block_diag_ones(CUMSUM_BLOCK, GLA_CHUNK)
    tril = _constant(same_chunk & (ci[None, :] <= ci[:, None]))
    triu = _constant(same_chunk & (ci[None, :] >= ci[:, None]))
    gmat = _constant(_block_diag_ones(cw, CONV_GROUP) / CONV_GROUP)

    (hb, qkv, gate, la), (wxkv,) = _in_proj(
        x2, row2(mix_norm[0]), w_in[0].T, w_af[0], w_ab[0], row2(b_af[0]), row2(b_ab[0]),
        (w_xkv[0],))
    (o_f, o_b), (wout, wxq, wxo, wup, wdown) = _gla(
        qkv, la, tril, triu, batch, seq, (w_out[0], w_xq[0], w_xo[0], w_up[0], w_down[0]))
    convw = jnp.transpose(conv_w, (1, 0, 2)).astype(F32)
    xa = _mix_xattn(x2, hb, gate, o_f, o_b, mem.reshape(batch * N_MEM, d), row2(mem_norm[0]), wxkv,
                    convw, row2(conv_norm[0]), gmat,
                    row2(gla_norm[0]), wout, row2(xa_norm[0]), wxq, wxo, seq)
    out = _mlp(xa, row2(mlp_norm[0]), wup, wdown, row2(final_norm))
    return out.reshape(batch, seq, d)
```
